```python
import jax, jax.numpy as jnp
from jax import lax
import numpy as np

D_MODEL = 4096
BATCH = 4
SEQ = 2048
DEPTH = 2
DEC_BATCH = 128
DEC_SEQ = 8
PAST_LEN = 16384
PAGE_SIZE = 128

A_HEADS = 16
A_DK = 128
A_DV = 128
A_QKV = A_HEADS * (2 * A_DK + A_DV)
B_HEADS = 32
B_HEADDIM = 64
B_DINNER = B_HEADS * B_HEADDIM
B_GROUPS = 4
B_REP = B_HEADS // B_GROUPS
B_DSTATE = 128
B_CONV_DIM = B_DINNER + 2 * B_GROUPS * B_DSTATE
C_HEADS = 8
C_DK = 256
C_DV = 256
SHORT_CONV = 4
CHUNK = 64
D_FF = 256 * ((8 * D_MODEL // 3 + 255) // 256)
FFN_CONV = 3
ROPE_BASE = 10000.0
EPS = 1e-6
IN_SPLITS = (A_QKV, A_HEADS * A_DV, A_HEADS, A_HEADS,
             B_DINNER, B_CONV_DIM, B_HEADS,
             C_HEADS * C_DK, C_HEADS * C_DK, C_HEADS * C_DV, C_HEADS * C_DV,
             3 * D_MODEL)
IN_TOTAL = sum(IN_SPLITS)

kernel_name = 'hybrid_gdn_ssd_retention_convffn'


def rmsnorm(x, gain=None):
    xf = x.astype(jnp.float32)
    y = xf * lax.rsqrt(jnp.mean(xf * xf, axis=-1, keepdims=True) + EPS)
    if gain is not None:
        y = y * gain.astype(jnp.float32)
    return y.astype(x.dtype)


def l2norm(x):
    xf = x.astype(jnp.float32)
    return (xf * lax.rsqrt(jnp.sum(xf * xf, axis=-1, keepdims=True) + EPS)).astype(x.dtype)


def causal_dwconv(x, buf, w, b=None):
    width, length = w.shape[0], x.shape[1]
    xc = jnp.concatenate([buf.astype(x.dtype), x], axis=1)
    y = xc[:, 0:length] * w[0]
    for j in range(1, width):
        y = y + xc[:, j:j + length] * w[j]
    if b is not None:
        y = y + b
    return y, xc[:, length:]


def chunk_len(length):
    return CHUNK if length % CHUNK == 0 else length


def to_chunks(t, c):
    bsz, length = t.shape[:2]
    return jnp.swapaxes(t.reshape((bsz, length // c, c) + t.shape[2:]), 0, 1)


def from_chunks(t):
    n, bsz, c = t.shape[:3]
    return jnp.swapaxes(t, 0, 1).reshape((bsz, n * c) + t.shape[3:])


def rotary(t, pos):
    half = t.shape[-1] // 2
    inv = ROPE_BASE ** (-jnp.arange(half, dtype=jnp.float32) / half)
    ang = pos.astype(jnp.float32)[:, None] * inv[None, :]
    cos, sin = jnp.cos(ang)[None, :, None, :], jnp.sin(ang)[None, :, None, :]
    tf = t.astype(jnp.float32)
    t1, t2 = tf[..., :half], tf[..., half:]
    return jnp.concatenate([t1 * cos - t2 * sin, t1 * sin + t2 * cos], axis=-1).astype(t.dtype)


def gated_delta_rule(q, k, v, g, beta, s0):
    f32 = jnp.float32
    c = chunk_len(q.shape[1])
    dv = v.shape[-1]
    tri = jnp.tril(jnp.ones((c, c), bool))
    strict = jnp.tril(jnp.ones((c, c), bool), -1)
    eye = jnp.eye(c, dtype=f32)
    gcum = jnp.cumsum(to_chunks(g.astype(f32), c), axis=2)
    xs = (to_chunks(q.astype(f32), c), to_chunks(k.astype(f32), c), to_chunks(v.astype(f32), c),
          to_chunks(beta.astype(f32), c), gcum)

    def step(s, inp):
        qi, ki, vi, bi, gi = inp
        gt = jnp.swapaxes(gi, 1, 2)
        decay = jnp.exp(jnp.where(tri, gt[..., :, None] - gt[..., None, :], -jnp.inf))
        kb = ki * bi[..., None]
        lmat = jnp.where(strict, jnp.einsum('bihk,bjhk->bhij', kb, ki) * decay, 0.0)
        rhs = jnp.concatenate([vi * bi[..., None], kb * jnp.exp(gi)[..., None]], axis=-1)
        sol = lax.linalg.triangular_solve(lmat + eye, jnp.swapaxes(rhs, 1, 2),
                                          left_side=True, lower=True, unit_diagonal=True)
        u, w = sol[..., :dv], sol[..., dv:]
        v_new = u - jnp.einsum('bhck,bhkv->bhcv', w, s)
        qk = jnp.where(tri, jnp.einsum('bihk,bjhk->bhij', qi, ki) * decay, 0.0)
        o = (jnp.einsum('bihk,bhkv->bihv', qi * jnp.exp(gi)[..., None], s)
             + jnp.einsum('bhij,bhjv->bihv', qk, v_new))
        g_last = gt[..., -1]
        s = (s * jnp.exp(g_last)[..., None, None]
             + jnp.einsum('bihk,bhiv->bhkv', ki * jnp.exp(g_last[:, None, :] - gi)[..., None], v_new))
        return s, o

    s_fin, o = lax.scan(step, s0.astype(f32), xs)
    return from_chunks(o).astype(v.dtype), s_fin.astype(s0.dtype)


def decay_linear_attention(q, k, v, g, s0):
    f32 = jnp.float32
    c = chunk_len(q.shape[1])
    tri = jnp.tril(jnp.ones((c, c), bool))
    gcum = jnp.cumsum(to_chunks(g.astype(f32), c), axis=2)
    xs = (to_chunks(q.astype(f32), c), to_chunks(k.astype(f32), c), to_chunks(v.astype(f32), c), gcum)

    def step(s, inp):
        qi, ki, vi, gi = inp
        gt = jnp.moveaxis(gi, 1, -1)
        decay = jnp.exp(jnp.where(tri, gt[..., :, None] - gt[..., None, :], -jnp.inf))
        scores = jnp.einsum('bigk,bjgk->bgij', qi, ki)[:, :, None] * decay
        o = (jnp.einsum('bgrij,bjgrv->bigrv', scores, vi)
             + jnp.einsum('bigk,bgrkv->bigrv', qi, s) * jnp.exp(gi)[..., None])
        g_last = gi[:, -1]
        s = (s * jnp.exp(g_last)[..., None, None]
             + jnp.einsum('bjgk,bjgrv->bgrkv', ki, vi * jnp.exp(g_last[:, None] - gi)[..., None]))
        return s, o

    s_fin, o = lax.scan(step, s0.astype(f32), xs)
    return from_chunks(o).astype(v.dtype), s_fin.astype(s0.dtype)


def state_shapes():
    return ((A_HEADS, A_DK, A_DV), (SHORT_CONV - 1, A_QKV),
            (B_HEADS, B_DSTATE, B_HEADDIM), (SHORT_CONV - 1, B_CONV_DIM),
            (C_HEADS, C_DK, C_DV), (FFN_CONV - 1, D_FF))


def hybrid_layer(x, st, p, pos0):
    s_gdn, c_gdn, s_ssm, c_ssm, s_ret, c_ffn = st
    bsz, length, _ = x.shape
    u = rmsnorm(x, p['norm_mix'])
    proj = u @ p['w_in']
    offs = np.cumsum(IN_SPLITS)[:-1].tolist()
    (a_qkv, a_z, a_b, a_a, b_z, b_xbc, b_dt, c_q, c_k, c_v, c_g, gates) = jnp.split(proj, offs, axis=-1)

    a_qkv, c_gdn_new = causal_dwconv(a_qkv, c_gdn, p['gdn_conv_w'])
    a_qkv = jax.nn.silu(a_qkv)
    aq, ak, av = jnp.split(a_qkv, [A_HEADS * A_DK, 2 * A_HEADS * A_DK], axis=-1)
    aq = l2norm(aq.reshape(bsz, length, A_HEADS, A_DK)) * (A_DK ** -0.5)
    ak = l2norm(ak.reshape(bsz, length, A_HEADS, A_DK))
    av = av.reshape(bsz, length, A_HEADS, A_DV)
    a_beta = jax.nn.sigmoid(a_b)
    a_g = -jnp.exp(p['gdn_a_log']) * jax.nn.softplus(a_a + p['gdn_dt_bias'])
    ao, s_gdn_new = gated_delta_rule(aq, ak, av, a_g, a_beta, s_gdn)
    ya = (rmsnorm(ao, p['gdn_norm']) * jax.nn.silu(a_z.reshape(bsz, length, A_HEADS, A_DV))).reshape(bsz, length, -1)

    xbc, c_ssm_new = causal_dwconv(b_xbc, c_ssm, p['ssm_conv_w'], p['ssm_conv_b'])
    xbc = jax.nn.silu(xbc)
    bx, bB, bC = jnp.split(xbc, [B_DINNER, B_DINNER + B_GROUPS * B_DSTATE], axis=-1)
    bx = bx.reshape(bsz, length, B_GROUPS, B_REP, B_HEADDIM)
    bB = bB.reshape(bsz, length, B_GROUPS, B_DSTATE)
    bC = bC.reshape(bsz, length, B_GROUPS, B_DSTATE)
    dt = jax.nn.softplus(b_dt + p['ssm_dt_bias']).reshape(bsz, length, B_GROUPS, B_REP)
    a_ssm = -jnp.exp(p['ssm_a_log']).reshape(B_GROUPS, B_REP)
    yb, s_ssm_new = decay_linear_attention(bC, bB, bx * dt[..., None], dt * a_ssm,
                                           s_ssm.reshape(bsz, B_GROUPS, B_REP, B_DSTATE, B_HEADDIM))
    yb = yb + p['ssm_d'].reshape(B_GROUPS, B_REP)[:, :, None] * bx
    yb = (yb.reshape(bsz, length, B_DINNER) * jax.nn.silu(b_z)).reshape(bsz, length, B_GROUPS, B_DINNER // B_GROUPS)
    yb = rmsnorm(yb, p['ssm_norm'].reshape(B_GROUPS, -1)).reshape(bsz, length, B_DINNER)
    s_ssm_new = s_ssm_new.reshape(bsz, B_HEADS, B_DSTATE, B_HEADDIM)

    pos = pos0 + jnp.arange(length)
    cq = rotary(c_q.reshape(bsz, length, C_HEADS, C_DK), pos)
    ck = rotary(c_k.reshape(bsz, length, C_HEADS, C_DK), pos) * (C_DK ** -0.5)
    cv = c_v.reshape(bsz, length, C_HEADS, 1, C_DV)
    log_gamma = jnp.log(1.0 - 2.0 ** (-5.0 - jnp.arange(C_HEADS, dtype=jnp.float32)))
    c_logdecay = jnp.broadcast_to(log_gamma[:, None], (bsz, length, C_HEADS, 1))
    yc, s_ret_new = decay_linear_attention(cq, ck, cv, c_logdecay, s_ret[:, :, None])
    yc = rmsnorm(yc[:, :, :, 0]).reshape(bsz, length, -1) * jax.nn.silu(c_g)
    s_ret_new = s_ret_new[:, :, 0]

    ga, gb, gc = jnp.split(jax.nn.sigmoid(gates), 3, axis=-1)
    h = ga * (ya @ p['w_branch_a']) + gb * (yb @ p['w_branch_b']) + gc * (yc @ p['w_branch_c'])
    x = x + h @ p['w_out']

    u2 = rmsnorm(x, p['norm_ffn'])
    fg, c_ffn_new = causal_dwconv(u2 @ p['w_ffn_gate'], c_ffn, p['ffn_conv_w'], p['ffn_conv_b'])
    x = x + (jax.nn.silu(fg) * (u2 @ p['w_ffn_up'])) @ p['w_ffn_down']
    return x, (s_gdn_new, c_gdn_new, s_ssm_new, c_ssm_new, s_ret_new, c_ffn_new)


def run_trunk(x, states, params, norm_final, pos0):
    new_states = []
    for layer in range(DEPTH):
        p = {name: w[layer] for name, w in params.items()}
        x, ns = hybrid_layer(x, tuple(s[layer] for s in states), p, pos0)
        new_states.append(ns)
    y = rmsnorm(x, norm_final)
    return y, [jnp.stack([ns[i] for ns in new_states]) for i in range(len(states))]


def setup_inputs(seed: int = 0) -> dict:
    key = jax.random.key(seed)
    ks = jax.random.split(key, 32)
    f32 = jnp.float32

    def nrm(i, shape, scale):
        return scale * jax.random.normal(ks[i], shape, f32)

    def dt_bias_init(i, shape):
        dt = jnp.exp(jax.random.uniform(ks[i], shape, f32, float(np.log(1e-3)), float(np.log(1e-1))))
        return dt + jnp.log(-jnp.expm1(-dt))

    def a_log_init(i, shape):
        return jnp.log(jax.random.uniform(ks[i], shape, f32, 1.0, 16.0))

    L = DEPTH
    return {
        'x_prompt': nrm(0, (BATCH, SEQ, D_MODEL), 1.0),
        'x_sample': nrm(1, (DEC_BATCH, DEC_SEQ, D_MODEL), 1.0),
        'state_gdn': nrm(2, (L, DEC_BATCH, A_HEADS, A_DK, A_DV), 0.1),
        'state_gdn_conv': nrm(3, (L, DEC_BATCH, SHORT_CONV - 1, A_QKV), 1.0),
        'state_ssm': nrm(4, (L, DEC_BATCH, B_HEADS, B_DSTATE, B_HEADDIM), 0.1),
        'state_ssm_conv': nrm(5, (L, DEC_BATCH, SHORT_CONV - 1, B_CONV_DIM), 1.0),
        'state_ret': nrm(6, (L, DEC_BATCH, C_HEADS, C_DK, C_DV), 1.0),
        'state_ffn_conv': nrm(7, (L, DEC_BATCH, FFN_CONV - 1, D_FF), 1.0),
        'norm_mix': 1.0 + nrm(8, (L, D_MODEL), 0.02),
        'w_in': nrm(9, (L, D_MODEL, IN_TOTAL), D_MODEL ** -0.5),
        'gdn_conv_w': nrm(10, (L, SHORT_CONV, A_QKV), SHORT_CONV ** -0.5),
        'gdn_a_log': a_log_init(11, (L, A_HEADS)),
        'gdn_dt_bias': dt_bias_init(12, (L, A_HEADS)),
        'gdn_norm': 1.0 + nrm(13, (L, A_DV), 0.02),
        'ssm_conv_w': nrm(14, (L, SHORT_CONV, B_CONV_DIM), SHORT_CONV ** -0.5),
        'ssm_conv_b': nrm(15, (L, B_CONV_DIM), 0.02),
        'ssm_a_log': a_log_init(16, (L, B_HEADS)),
        'ssm_dt_bias': dt_bias_init(17, (L, B_HEADS)),
        'ssm_d': 1.0 + nrm(18, (L, B_HEADS), 0.1),
        'ssm_norm': 1.0 + nrm(19, (L, B_DINNER), 0.02),
        'w_branch_a': nrm(20, (L, A_HEADS * A_DV, D_MODEL), (A_HEADS * A_DV) ** -0.5),
        'w_branch_b': nrm(21, (L, B_DINNER, D_MODEL), B_DINNER ** -0.5),
        'w_branch_c': nrm(22, (L, C_HEADS * C_DV, D_MODEL), (C_HEADS * C_DV) ** -0.5),
        'w_out': nrm(23, (L, D_MODEL, D_MODEL), D_MODEL ** -0.5),
        'norm_ffn': 1.0 + nrm(24, (L, D_MODEL), 0.02),
        'w_ffn_gate': nrm(25, (L, D_MODEL, D_FF), D_MODEL ** -0.5),
        'w_ffn_up': nrm(26, (L, D_MODEL, D_FF), D_MODEL ** -0.5),
        'ffn_conv_w': nrm(27, (L, FFN_CONV, D_FF), FFN_CONV ** -0.5),
        'ffn_conv_b': nrm(28, (L, D_FF), 0.02),
        'w_ffn_down': nrm(29, (L, D_FF, D_MODEL), D_FF ** -0.5),
        'norm_final': 1.0 + nrm(30, (D_MODEL,), 0.02),
    }


def reference(x_prompt, x_sample, state_gdn, state_gdn_conv, state_ssm, state_ssm_conv, state_ret, state_ffn_conv,
              norm_mix, w_in, gdn_conv_w, gdn_a_log, gdn_dt_bias, gdn_norm,
              ssm_conv_w, ssm_conv_b, ssm_a_log, ssm_dt_bias, ssm_d, ssm_norm,
              w_branch_a, w_branch_b, w_branch_c, w_out,
              norm_ffn, w_ffn_gate, w_ffn_up, ffn_conv_w, ffn_conv_b, w_ffn_down, norm_final):
    params = dict(norm_mix=norm_mix, w_in=w_in, gdn_conv_w=gdn_conv_w, gdn_a_log=gdn_a_log,
                  gdn_dt_bias=gdn_dt_bias, gdn_norm=gdn_norm, ssm_conv_w=ssm_conv_w, ssm_conv_b=ssm_conv_b,
                  ssm_a_log=ssm_a_log, ssm_dt_bias=ssm_dt_bias, ssm_d=ssm_d, ssm_norm=ssm_norm,
                  w_branch_a=w_branch_a, w_branch_b=w_branch_b, w_branch_c=w_branch_c, w_out=w_out,
                  norm_ffn=norm_ffn, w_ffn_gate=w_ffn_gate, w_ffn_up=w_ffn_up, ffn_conv_w=ffn_conv_w,
                  ffn_conv_b=ffn_conv_b, w_ffn_down=w_ffn_down)
    prompt_states = tuple(jnp.zeros((DEPTH, BATCH) + shape, x_prompt.dtype) for shape in state_shapes())
    y_prompt, (gdn_p, gdn_conv_p, ssm_p, ssm_conv_p, ret_p, ffn_conv_p) = run_trunk(
        x_prompt, prompt_states, params, norm_final, 0)
    sample_states = (state_gdn, state_gdn_conv, state_ssm, state_ssm_conv, state_ret, state_ffn_conv)
    y_sample, (gdn_s, gdn_conv_s, ssm_s, ssm_conv_s, ret_s, ffn_conv_s) = run_trunk(
        x_sample, sample_states, params, norm_final, PAST_LEN)
    return (y_prompt, y_sample,
            gdn_p, gdn_conv_p, ssm_p, ssm_conv_p, ret_p, ffn_conv_p,
            gdn_s, gdn_conv_s, ssm_s, ssm_conv_s, ret_s, ffn_conv_s)
```

```python
import functools

import numpy as np
import jax
import jax.numpy as jnp
from jax import lax
from jax.experimental import pallas as pl
from jax.experimental.pallas import tpu as pltpu

f32 = jnp.float32
bf16 = jnp.bfloat16

D_MODEL = 4096
A_HEADS, A_DK, A_DV = 16, 128, 128
A_QKV = A_HEADS * (2 * A_DK + A_DV)
B_HEADS, B_HEADDIM, B_GROUPS, B_DSTATE = 32, 64, 4, 128
B_DINNER = B_HEADS * B_HEADDIM
B_REP = B_HEADS // B_GROUPS
B_CONV_DIM = B_DINNER + 2 * B_GROUPS * B_DSTATE
C_HEADS, C_DK, C_DV = 8, 256, 256
SHORT_CONV = 4
FFN_CONV = 3
CHUNK = 64
D_FF = 11008
ROPE_BASE = 10000.0
EPS = 1e-6
PAST_LEN = 16384

OFF_QKV = 0
OFF_AZ = 6144
OFF_BZ = 8192
OFF_BX = 10240
OFF_CQ = 12288
OFF_CK = 14336
OFF_CV = 16384
OFF_CG = 18432
OFF_GATES = 20480
OFF_BB = 32768
OFF_BC = 33280
OFF_SMALL = 33792
NP = 34304
LANE_AB, LANE_AA, LANE_BDT = 0, 16, 32

VMEM_LIMIT = 56 * 1024 * 1024
NEG_BIG = -1e30


def _cparams(*sem):
    return pltpu.CompilerParams(dimension_semantics=sem, vmem_limit_bytes=VMEM_LIMIT)


def _tile(n, pref, mult):
    t = min(pref, n)
    t -= t % mult
    while t >= mult:
        if n % t == 0:
            return t
        t -= mult
    return n


def _sigmoid(x):
    return 1.0 / (1.0 + jnp.exp(-x))


def _silu(x):
    return x * _sigmoid(x)


def _softplus(x):
    return jnp.maximum(x, 0.0) + jnp.log(1.0 + jnp.exp(-jnp.abs(x)))


def _dot(a, b):
    return jnp.dot(a.astype(bf16), b.astype(bf16), preferred_element_type=f32)


def _dot_nt(a, b):
    return lax.dot_general(a.astype(bf16), b.astype(bf16), (((1,), (1,)), ((), ())), preferred_element_type=f32)


def _dot_tn(a, b):
    return lax.dot_general(a.astype(bf16), b.astype(bf16), (((0,), (0,)), ((), ())), preferred_element_type=f32)


def _split(a):
    hi = a.astype(bf16)
    lo = (a - hi.astype(f32)).astype(bf16)
    return hi, lo


def _dot3(a, b):
    ah, al = _split(a)
    bh, bl = _split(b)
    d = functools.partial(jnp.dot, preferred_element_type=f32)
    return d(ah, bh) + (d(ah, bl) + d(al, bh))


def _cumsum_rows(x, n):
    row = lax.broadcasted_iota(jnp.int32, x.shape, 0)
    s = 1
    while s < n:
        x = x + jnp.where(row >= s, pltpu.roll(x, s, axis=0), 0.0)
        s *= 2
    return x


def _col_to_row(col, eye):
    return jnp.sum(jnp.where(eye, col, 0.0), axis=0, keepdims=True)


def _inv_unit_lower(lmat, ii, jj, n):
    eye = (ii == jj).astype(f32)
    p = jnp.where((ii >> 3) == (jj >> 3), -lmat, 0.0)
    t = eye + p
    p2 = _dot3(p, p)
    t = t + _dot3(t, p2)
    p4 = _dot3(p2, p2)
    t = t + _dot3(t, p4)
    s = 8
    while s < n:
        sh = s.bit_length() - 1
        off = jnp.where(((ii >> (sh + 1)) == (jj >> (sh + 1))) & ((ii >> sh) != (jj >> sh)), lmat, 0.0)
        t = t - _dot3(t, _dot3(off, t))
        s *= 2
    return t


def _rmsnorm_kernel(x_ref, g_ref, o_ref):
    x = x_ref[...]
    y = x * lax.rsqrt(jnp.mean(x * x, axis=-1, keepdims=True) + EPS) * g_ref[...]
    o_ref[...] = y.astype(o_ref.dtype)


def _rmsnorm(x, gain, out_dtype):
    t, d = x.shape
    tm = _tile(t, 256, 16)
    return pl.pallas_call(
        _rmsnorm_kernel,
        grid=(t // tm,),
        in_specs=[pl.BlockSpec((tm, d), lambda i: (i, 0)), pl.BlockSpec((1, d), lambda i: (0, 0))],
        out_specs=pl.BlockSpec((tm, d), lambda i: (i, 0)),
        out_shape=jax.ShapeDtypeStruct((t, d), out_dtype),
        compiler_params=_cparams("parallel"),
        name="rmsnorm",
    )(x, gain.reshape(1, d))


def _mm_kernel(nk, has_res, *refs):
    if has_res:
        a_ref, b_ref, r_ref, o_ref = refs[:4]
    else:
        a_ref, b_ref, o_ref = refs[:3]
        r_ref = None
    if nk == 1:
        acc = jnp.dot(a_ref[...], b_ref[...], preferred_element_type=f32)
        if has_res:
            acc = r_ref[...] + acc
        o_ref[...] = acc.astype(o_ref.dtype)
        return
    acc_ref = refs[-1]
    k = pl.program_id(2)

    @pl.when(k == 0)
    def _():
        acc_ref[...] = jnp.zeros_like(acc_ref)

    acc_ref[...] += jnp.dot(a_ref[...], b_ref[...], preferred_element_type=f32)

    @pl.when(k == nk - 1)
    def _():
        acc = acc_ref[...]
        if has_res:
            acc = r_ref[...] + acc
        o_ref[...] = acc.astype(o_ref.dtype)


def _matmul(a, b, res=None, *, tm=1024, tn=512, tk=None, out_dtype=f32, name="matmul"):
    m, kdim = a.shape
    n = b.shape[1]
    tm = _tile(m, tm, 16)
    tn = _tile(n, tn, 128)
    tk = kdim if tk is None else tk
    nk = kdim // tk
    in_specs = [pl.BlockSpec((tm, tk), lambda i, j, k: (i, k)), pl.BlockSpec((tk, tn), lambda i, j, k: (k, j))]
    args = [a, b]
    if res is not None:
        in_specs.append(pl.BlockSpec((tm, tn), lambda i, j, k: (i, j)))
        args.append(res)
    return pl.pallas_call(
        functools.partial(_mm_kernel, nk, res is not None),
        grid=(m // tm, n // tn, nk),
        in_specs=in_specs,
        out_specs=pl.BlockSpec((tm, tn), lambda i, j, k: (i, j)),
        out_shape=jax.ShapeDtypeStruct((m, n), out_dtype),
        scratch_shapes=[pltpu.VMEM((tm, tn), f32)] if nk > 1 else [],
        compiler_params=_cparams("parallel", "parallel", "arbitrary"),
        name=name,
    )(*args)


def _merge_kernel(ya_ref, yb_ref, yc_ref, wa_ref, wb_ref, wc_ref, ga_ref, gb_ref, gc_ref, o_ref):
    d = functools.partial(jnp.dot, preferred_element_type=f32)
    h = _sigmoid(ga_ref[...]) * d(ya_ref[...], wa_ref[...])
    h = h + _sigmoid(gb_ref[...]) * d(yb_ref[...], wb_ref[...])
    h = h + _sigmoid(gc_ref[...]) * d(yc_ref[...], wc_ref[...])
    o_ref[...] = h.astype(o_ref.dtype)


def _merge(ya, yb, yc, wa, wb, wc, proj):
    t, kd = ya.shape
    n = wa.shape[1]
    tm = _tile(t, 512, 16)
    tn = 512
    g0 = OFF_GATES // tn
    gstep = n // tn
    yspec = pl.BlockSpec((tm, kd), lambda i, j: (i, 0))
    wspec = pl.BlockSpec((kd, tn), lambda i, j: (0, j))
    gspecs = [pl.BlockSpec((tm, tn), functools.partial(lambda i, j, o: (i, o + j), o=g0 + r * gstep)) for r in range(3)]
    return pl.pallas_call(
        _merge_kernel,
        grid=(t // tm, n // tn),
        in_specs=[yspec, yspec, yspec, wspec, wspec, wspec] + gspecs,
        out_specs=pl.BlockSpec((tm, tn), lambda i, j: (i, j)),
        out_shape=jax.ShapeDtypeStruct((t, n), bf16),
        compiler_params=_cparams("parallel", "parallel"),
        name="merge",
    )(ya, yb, yc, wa, wb, wc, proj, proj, proj)


def _gdn_kernel(C, has_state, *refs):
    if has_state:
        qkv_ref, z_ref, sm_ref, cw_ref, par_ref, gn_ref, s0_ref, cs_ref, y_ref, so_ref, cso_ref, ext_ref = refs
    else:
        qkv_ref, z_ref, sm_ref, cw_ref, par_ref, gn_ref, y_ref, so_ref, cso_ref, ext_ref = refs
    c = pl.program_id(1)

    @pl.when(c == 0)
    def _():
        ext_ref[0:8, :] = jnp.zeros((8, A_QKV), f32)
        if has_state:
            so_ref[...] = s0_ref[...]
            ext_ref[5:8, :] = cs_ref[0]
        else:
            so_ref[...] = jnp.zeros_like(so_ref)

    ext_ref[8:8 + C, :] = qkv_ref[...]

    def conv_silu(lo):
        acc = cw_ref[0:1, lo:lo + 128] * ext_ref[pl.ds(5, C), lo:lo + 128]
        for j in range(1, SHORT_CONV):
            acc = acc + cw_ref[j:j + 1, lo:lo + 128] * ext_ref[pl.ds(5 + j, C), lo:lo + 128]
        return _silu(acc)

    sm = sm_ref[...]
    beta_all = _sigmoid(sm)
    g_all = -jnp.exp(par_ref[0:1, :]) * _softplus(sm + par_ref[1:2, :])
    gc_all = _cumsum_rows(g_all, C)
    egc_all = jnp.exp(gc_all)
    gl_all = gc_all[C - 1:C, :]
    egl_all = jnp.exp(gl_all)
    edl_all = jnp.exp(gl_all - gc_all)

    ii = lax.broadcasted_iota(jnp.int32, (C, C), 0)
    jj = lax.broadcasted_iota(jnp.int32, (C, C), 1)
    eye = ii == jj
    tri = ii >= jj
    strict = ii > jj
    gain = gn_ref[...]

    for h in range(A_HEADS):
        q = conv_silu(h * A_DK)
        k = conv_silu(A_HEADS * A_DK + h * A_DK)
        v = conv_silu(2 * A_HEADS * A_DK + h * A_DV)
        q = q * lax.rsqrt(jnp.sum(q * q, axis=-1, keepdims=True) + EPS) * (A_DK ** -0.5)
        k = k * lax.rsqrt(jnp.sum(k * k, axis=-1, keepdims=True) + EPS)
        beta = beta_all[:, LANE_AB + h:LANE_AB + h + 1]
        gcol = gc_all[:, LANE_AA + h:LANE_AA + h + 1]
        egc = egc_all[:, LANE_AA + h:LANE_AA + h + 1]
        edl = edl_all[:, LANE_AA + h:LANE_AA + h + 1]
        egl = egl_all[:, LANE_AA + h:LANE_AA + h + 1]
        grow = _col_to_row(gcol, eye)
        decay = jnp.exp(jnp.where(tri, gcol - grow, NEG_BIG))
        kb = k * beta
        lmat = jnp.where(strict, _dot_nt(kb, k) * decay, 0.0)
        rhs = jnp.concatenate([v * beta, kb * egc], axis=-1)
        sol = _dot3(_inv_unit_lower(lmat, ii, jj, C), rhs)
        u, w = sol[:, :A_DV], sol[:, A_DV:]
        s = so_ref[0, h]
        v_new = u - _dot(w, s)
        qk = jnp.where(tri, _dot_nt(q, k) * decay, 0.0)
        o = _dot(q * egc, s) + _dot(qk, v_new)
        so_ref[0, h] = s * egl + _dot_tn(k * edl, v_new)
        o = o * lax.rsqrt(jnp.mean(o * o, axis=-1, keepdims=True) + EPS) * gain
        y_ref[:, h * A_DV:(h + 1) * A_DV] = o * _silu(z_ref[:, h * A_DV:(h + 1) * A_DV])

    cso_ref[0] = ext_ref[C + 5:C + 8, :]
    ext_ref[0:8, :] = ext_ref[C:C + 8, :]


def _gdn(proj, cw, par, gn, s0, cs, *, row0, nb, L, C):
    nch = L // C
    rb0 = row0 // C
    has_state = s0 is not None

    def rowmap(col):
        return lambda b, c: (rb0 + b * nch + c, col)

    in_specs = [
        pl.BlockSpec((C, A_QKV), rowmap(0)),
        pl.BlockSpec((C, 2048), rowmap(OFF_AZ // 2048)),
        pl.BlockSpec((C, 128), rowmap(OFF_SMALL // 128)),
        pl.BlockSpec((SHORT_CONV, A_QKV), lambda b, c: (0, 0)),
        pl.BlockSpec((2, 128), lambda b, c: (0, 0)),
        pl.BlockSpec((1, A_DV), lambda b, c: (0, 0)),
    ]
    args = [proj, proj, proj, cw, par, gn]
    st_spec = pl.BlockSpec((1, A_HEADS, A_DK, A_DV), lambda b, c: (b, 0, 0, 0))
    cs_spec = pl.BlockSpec((1, SHORT_CONV - 1, A_QKV), lambda b, c: (b, 0, 0))
    if has_state:
        in_specs += [st_spec, cs_spec]
        args += [s0, cs]
    return pl.pallas_call(
        functools.partial(_gdn_kernel, C, has_state),
        grid=(nb, nch),
        in_specs=in_specs,
        out_specs=[pl.BlockSpec((C, 2048), lambda b, c: (b * nch + c, 0)), st_spec, cs_spec],
        out_shape=[jax.ShapeDtypeStruct((nb * L, 2048), f32),
                   jax.ShapeDtypeStruct((nb, A_HEADS, A_DK, A_DV), f32),
                   jax.ShapeDtypeStruct((nb, SHORT_CONV - 1, A_QKV), f32)],
        scratch_shapes=[pltpu.VMEM((C + 8, A_QKV), f32)],
        compiler_params=_cparams("parallel", "arbitrary"),
        name="gdn_state" if has_state else "gdn",
    )(*args)


def _ssd_kernel(C, has_state, *refs):
    if has_state:
        (z_ref, x_ref, bb_ref, bc_ref, sm_ref, cw_ref, cb_ref, par_ref, gn_ref, s0_ref, cs_ref,
         y_ref, so_ref, cso_ref, ext_ref) = refs
    else:
        (z_ref, x_ref, bb_ref, bc_ref, sm_ref, cw_ref, cb_ref, par_ref, gn_ref,
         y_ref, so_ref, cso_ref, ext_ref) = refs
    c = pl.program_id(1)
    ng = B_GROUPS * B_DSTATE

    @pl.when(c == 0)
    def _():
        ext_ref[0:8, :] = jnp.zeros((8, B_CONV_DIM), f32)
        if has_state:
            so_ref[...] = s0_ref[...]
            ext_ref[5:8, :] = cs_ref[0]
        else:
            so_ref[...] = jnp.zeros_like(so_ref)

    ext_ref[8:8 + C, 0:B_DINNER] = x_ref[...]
    ext_ref[8:8 + C, B_DINNER:B_DINNER + ng] = bb_ref[...]
    ext_ref[8:8 + C, B_DINNER + ng:B_CONV_DIM] = bc_ref[...]

    def conv_silu(lo):
        acc = cw_ref[0:1, lo:lo + 128] * ext_ref[pl.ds(5, C), lo:lo + 128]
        for j in range(1, SHORT_CONV):
            acc = acc + cw_ref[j:j + 1, lo:lo + 128] * ext_ref[pl.ds(5 + j, C), lo:lo + 128]
        return _silu(acc + cb_ref[0:1, lo:lo + 128])

    sm = sm_ref[...]
    dt_all = _softplus(sm + par_ref[1:2, :])
    g_all = dt_all * (-jnp.exp(par_ref[0:1, :]))
    gc_all = _cumsum_rows(g_all, C)
    egc_all = jnp.exp(gc_all)
    gl_all = gc_all[C - 1:C, :]
    egl_all = jnp.exp(gl_all)
    edl_all = jnp.exp(gl_all - gc_all)
    d_all = par_ref[2:3, :]

    ii = lax.broadcasted_iota(jnp.int32, (C, C), 0)
    jj = lax.broadcasted_iota(jnp.int32, (C, C), 1)
    eye = ii == jj
    tri = ii >= jj
    first = lax.broadcasted_iota(jnp.int32, (C, 128), 1) < B_HEADDIM
    first_row = lax.broadcasted_iota(jnp.int32, (1, 128), 1) < B_HEADDIM

    def pick(arr, l0, l1, rows=None):
        a0 = arr[:, l0:l0 + 1]
        a1 = arr[:, l1:l1 + 1]
        return jnp.where(first if rows is None else first_row, a0, a1)

    pairs_per_group = B_REP // 2
    for g in range(B_GROUPS):
        bg = conv_silu(B_DINNER + g * B_DSTATE)
        cg = conv_silu(B_DINNER + ng + g * B_DSTATE)
        scores = _dot_nt(cg, bg)
        ys = []
        ssq = jnp.zeros((C, 1), f32)
        for pp in range(pairs_per_group):
            p = g * pairs_per_group + pp
            l0 = LANE_BDT + 2 * p
            l1 = l0 + 1
            xp = conv_silu(p * 128)
            v = xp * pick(dt_all, l0, l1)
            vb = v.astype(bf16)
            sds = []
            for l in (l0, l1):
                gcol = gc_all[:, l:l + 1]
                grow = _col_to_row(gcol, eye)
                decay = jnp.exp(jnp.where(tri, gcol - grow, NEG_BIG))
                sds.append(jnp.dot((scores * decay).astype(bf16), vb, preferred_element_type=f32))
            s = so_ref[0, p]
            o = jnp.where(first, sds[0], sds[1]) + _dot(cg, s) * pick(egc_all, l0, l1)
            so_ref[0, p] = s * pick(egl_all, l0, l1, rows=1) + _dot_tn(bg, v * pick(edl_all, l0, l1))
            yv = o + pick(d_all, l0, l1, rows=1) * xp
            yv = yv * _silu(z_ref[:, p * 128:(p + 1) * 128])
            ssq = ssq + jnp.sum(yv * yv, axis=-1, keepdims=True)
            ys.append(yv)
        scale = lax.rsqrt(ssq * (1.0 / (B_DINNER // B_GROUPS)) + EPS)
        for pp in range(pairs_per_group):
            p = g * pairs_per_group + pp
            y_ref[:, p * 128:(p + 1) * 128] = ys[pp] * scale * gn_ref[0:1, p * 128:(p + 1) * 128]

    cso_ref[0] = ext_ref[C + 5:C + 8, :]
    ext_ref[0:8, :] = ext_ref[C:C + 8, :]


def _ssd(proj, cw, cb, par, gn, s0, cs, *, row0, nb, L, C):
    nch = L // C
    rb0 = row0 // C
    has_state = s0 is not None
    ng = B_GROUPS * B_DSTATE

    def rowmap(col):
        return lambda b, c: (rb0 + b * nch + c, col)

    in_specs = [
        pl.BlockSpec((C, 2048), rowmap(OFF_BZ // 2048)),
        pl.BlockSpec((C, 2048), rowmap(OFF_BX // 2048)),
        pl.BlockSpec((C, ng), rowmap(OFF_BB // ng)),
        pl.BlockSpec((C, ng), rowmap(OFF_BC // ng)),
        pl.BlockSpec((C, 128), rowmap(OFF_SMALL // 128)),
        pl.BlockSpec((SHORT_CONV, B_CONV_DIM), lambda b, c: (0, 0)),
        pl.BlockSpec((1, B_CONV_DIM), lambda b, c: (0, 0)),
        pl.BlockSpec((3, 128), lambda b, c: (0, 0)),
        pl.BlockSpec((1, B_DINNER), lambda b, c: (0, 0)),
    ]
    args = [proj, proj, proj, proj, proj, cw, cb, par, gn]
    st_spec = pl.BlockSpec((1, B_HEADS // 2, B_DSTATE, 128), lambda b, c: (b, 0, 0, 0))
    cs_spec = pl.BlockSpec((1, SHORT_CONV - 1, B_CONV_DIM), lambda b, c: (b, 0, 0))
    if has_state:
        in_specs += [st_spec, cs_spec]
        args += [s0, cs]
    return pl.pallas_call(
        functools.partial(_ssd_kernel, C, has_state),
        grid=(nb, nch),
        in_specs=in_specs,
        out_specs=[pl.BlockSpec((C, 2048), lambda b, c: (b * nch + c, 0)), st_spec, cs_spec],
        out_shape=[jax.ShapeDtypeStruct((nb * L, 2048), f32),
                   jax.ShapeDtypeStruct((nb, B_HEADS // 2, B_DSTATE, 128), f32),
                   jax.ShapeDtypeStruct((nb, SHORT_CONV - 1, B_CONV_DIM), f32)],
        scratch_shapes=[pltpu.VMEM((C + 8, B_CONV_DIM), f32)],
        compiler_params=_cparams("parallel", "arbitrary"),
        name="ssd_state" if has_state else "ssd",
    )(*args)


def _pack_ssm_state(s):
    nb = s.shape[0]
    s = s.reshape(nb, B_HEADS // 2, 2, B_DSTATE, B_HEADDIM)
    return jnp.swapaxes(s, 2, 3).reshape(nb, B_HEADS // 2, B_DSTATE, 2 * B_HEADDIM)


def _unpack_ssm_state(s):
    nb = s.shape[0]
    s = s.reshape(nb, B_HEADS // 2, B_DSTATE, 2, B_HEADDIM)
    return jnp.swapaxes(s, 2, 3).reshape(nb, B_HEADS, B_DSTATE, B_HEADDIM)


_LOG_GAMMA = [float(np.log(np.float32(1.0) - np.float32(2.0) ** np.float32(-5.0 - h))) for h in range(C_HEADS)]


def _ret_kernel(C, has_state, pos0, *refs):
    if has_state:
        q_ref, k_ref, v_ref, g_ref, inv_ref, s0_ref, y_ref, so_ref = refs
    else:
        q_ref, k_ref, v_ref, g_ref, inv_ref, y_ref, so_ref = refs
    c = pl.program_id(1)

    @pl.when(c == 0)
    def _():
        if has_state:
            so_ref[...] = s0_ref[...]
        else:
            so_ref[...] = jnp.zeros_like(so_ref)

    half = C_DK // 2
    pos = (pos0 + c * C + lax.broadcasted_iota(jnp.int32, (C, half), 0)).astype(f32)
    ang = pos * inv_ref[...]
    cos = jnp.cos(ang)
    sin = jnp.sin(ang)

    def rot(ref, h):
        t1 = ref[:, h * C_DK:h * C_DK + half]
        t2 = ref[:, h * C_DK + half:(h + 1) * C_DK]
        return jnp.concatenate([t1 * cos - t2 * sin, t1 * sin + t2 * cos], axis=-1)

    ii = lax.broadcasted_iota(jnp.int32, (C, C), 0)
    jj = lax.broadcasted_iota(jnp.int32, (C, C), 1)
    tri = ii >= jj
    dij = (ii - jj).astype(f32)
    ipos = (lax.broadcasted_iota(jnp.int32, (C, 1), 0) + 1).astype(f32)

    for h in range(C_HEADS):
        lg = _LOG_GAMMA[h]
        q = rot(q_ref, h)
        k = rot(k_ref, h) * (C_DK ** -0.5)
        v = v_ref[:, h * C_DV:(h + 1) * C_DV]
        decay = jnp.exp(jnp.where(tri, dij * lg, NEG_BIG))
        egc = jnp.exp(ipos * lg)
        edl = jnp.exp((float(C) - ipos) * lg)
        egl = float(np.exp(np.float32(C * lg)))
        s = so_ref[0, h]
        o = _dot(_dot_nt(q, k) * decay, v) + _dot(q, s) * egc
        so_ref[0, h] = s * egl + _dot_tn(k, v * edl)
        o = o * lax.rsqrt(jnp.mean(o * o, axis=-1, keepdims=True) + EPS)
        y_ref[:, h * C_DV:(h + 1) * C_DV] = o * _silu(g_ref[:, h * C_DV:(h + 1) * C_DV])


def _ret(proj, inv, s0, *, row0, nb, L, C, pos0):
    nch = L // C
    rb0 = row0 // C
    has_state = s0 is not None

    def rowmap(col):
        return lambda b, c: (rb0 + b * nch + c, col)

    in_specs = [pl.BlockSpec((C, 2048), rowmap(off // 2048)) for off in (OFF_CQ, OFF_CK, OFF_CV, OFF_CG)]
    in_specs.append(pl.BlockSpec((1, C_DK // 2), lambda b, c: (0, 0)))
    args = [proj, proj, proj, proj, inv]
    st_spec = pl.BlockSpec((1, C_HEADS, C_DK, C_DV), lambda b, c: (b, 0, 0, 0))
    if has_state:
        in_specs.append(st_spec)
        args.append(s0)
    return pl.pallas_call(
        functools.partial(_ret_kernel, C, has_state, pos0),
        grid=(nb, nch),
        in_specs=in_specs,
        out_specs=[pl.BlockSpec((C, 2048), lambda b, c: (b * nch + c, 0)), st_spec],
        out_shape=[jax.ShapeDtypeStruct((nb * L, 2048), f32),
                   jax.ShapeDtypeStruct((nb, C_HEADS, C_DK, C_DV), f32)],
        compiler_params=_cparams("parallel", "arbitrary"),
        name="ret_state" if has_state else "ret",
    )(*args)


def _ffn_act_kernel(L, has_state, *refs):
    if has_state:
        g_ref, u_ref, w_ref, b_ref, cs_ref, a_ref, cso_ref, ext_ref = refs
    else:
        g_ref, u_ref, w_ref, b_ref, a_ref, cso_ref, ext_ref = refs
    n = ext_ref.shape[1]
    ext_ref[0:8, :] = jnp.zeros((8, n), f32)
    if has_state:
        ext_ref[6:8, :] = cs_ref[0]
    ext_ref[8:8 + L, :] = g_ref[...]
    acc = w_ref[0:1, :] * ext_ref[pl.ds(6, L), :]
    for j in range(1, FFN_CONV):
        acc = acc + w_ref[j:j + 1, :] * ext_ref[pl.ds(6 + j, L), :]
    acc = acc + b_ref[...]
    a_ref[...] = (_silu(acc) * u_ref[...]).astype(a_ref.dtype)
    cso_ref[0] = ext_ref[L + 6:L + 8, :]


def _ffn_act(gate, up, w, b, cs, *, row0, nb, L, tn, out_dtype):
    has_state = cs is not None
    rb0 = row0 // L
    nf = gate.shape[1]
    in_specs = [
        pl.BlockSpec((L, tn), lambda i, j: (rb0 + i, j)),
        pl.BlockSpec((L, tn), lambda i, j: (rb0 + i, j)),
        pl.BlockSpec((FFN_CONV, tn), lambda i, j: (0, j)),
        pl.BlockSpec((1, tn), lambda i, j: (0, j)),
    ]
    args = [gate, up, w, b]
    cs_spec = pl.BlockSpec((1, FFN_CONV - 1, tn), lambda i, j: (i, 0, j))
    if has_state:
        in_specs.append(cs_spec)
        args.append(cs)
    return pl.pallas_call(
        functools.partial(_ffn_act_kernel, L, has_state),
        grid=(nb, nf // tn),
        in_specs=in_specs,
        out_specs=[pl.BlockSpec((L, tn), lambda i, j: (i, j)), cs_spec],
        out_shape=[jax.ShapeDtypeStruct((nb * L, nf), out_dtype),
                   jax.ShapeDtypeStruct((nb, FFN_CONV - 1, nf), f32)],
        scratch_shapes=[pltpu.VMEM((L + 8, tn), f32)],
        compiler_params=_cparams("parallel", "parallel"),
        name="ffn_act_state" if has_state else "ffn_act",
    )(*args)


def _reorder_w_in(w):
    o_ab = A_QKV + A_HEADS * A_DV
    o_bz = o_ab + 2 * A_HEADS
    o_bx = o_bz + B_DINNER
    o_bbc = o_bx + B_DINNER
    o_dt = o_bx + B_CONV_DIM
    o_cq = o_dt + B_HEADS
    total = w.shape[1]
    parts = [w[:, 0:o_ab], w[:, o_bz:o_bbc], w[:, o_cq:total], w[:, o_bbc:o_dt], w[:, o_ab:o_bz], w[:, o_dt:o_cq]]
    used = sum(p.shape[1] for p in parts)
    parts.append(jnp.zeros((w.shape[0], NP - used), w.dtype))
    return jnp.concatenate(parts, axis=1).astype(bf16)


def _lane_row(vals, lane0):
    return jnp.zeros((128,), f32).at[lane0:lane0 + vals.shape[0]].set(vals.astype(f32))


def kernel(x_prompt, x_sample, state_gdn, state_gdn_conv, state_ssm, state_ssm_conv, state_ret, state_ffn_conv, norm_mix, w_in, gdn_conv_w, gdn_a_log, gdn_dt_bias, gdn_norm, ssm_conv_w, ssm_conv_b, ssm_a_log, ssm_dt_bias, ssm_d, ssm_norm, w_branch_a, w_branch_b, w_branch_c, w_out, norm_ffn, w_ffn_gate, w_ffn_up, ffn_conv_w, ffn_conv_b, w_ffn_down, norm_final):
    nbp, lp, d = x_prompt.shape
    nbs, ls, _ = x_sample.shape
    depth = w_in.shape[0]
    tp = nbp * lp
    ts = nbs * ls
    cp = CHUNK if lp % CHUNK == 0 else lp
    cs_len = CHUNK if ls % CHUNK == 0 else ls

    x = jnp.concatenate([x_prompt.reshape(tp, d), x_sample.reshape(ts, d)], axis=0)
    half = C_DK // 2
    inv = (ROPE_BASE ** (-jnp.arange(half, dtype=f32) / half)).reshape(1, half)

    outs_p = [[] for _ in range(6)]
    outs_s = [[] for _ in range(6)]
    for l in range(depth):
        u = _rmsnorm(x, norm_mix[l], bf16)
        proj = _matmul(u, _reorder_w_in(w_in[l]), name="in_proj")

        gdn_par = jnp.stack([_lane_row(gdn_a_log[l], LANE_AA), _lane_row(gdn_dt_bias[l], LANE_AA)])
        gdn_gn = gdn_norm[l].reshape(1, A_DV)
        ya_p, sa_p, ca_p = _gdn(proj, gdn_conv_w[l], gdn_par, gdn_gn, None, None, row0=0, nb=nbp, L=lp, C=cp)
        ya_s, sa_s, ca_s = _gdn(proj, gdn_conv_w[l], gdn_par, gdn_gn, state_gdn[l], state_gdn_conv[l],
                                row0=tp, nb=nbs, L=ls, C=cs_len)

        ssd_par = jnp.stack([_lane_row(ssm_a_log[l], LANE_BDT), _lane_row(ssm_dt_bias[l], LANE_BDT),
                             _lane_row(ssm_d[l], LANE_BDT)])
        ssd_cb = ssm_conv_b[l].reshape(1, B_CONV_DIM)
        ssd_gn = ssm_norm[l].reshape(1, B_DINNER)
        yb_p, sb_p, cb_p = _ssd(proj, ssm_conv_w[l], ssd_cb, ssd_par, ssd_gn, None, None, row0=0, nb=nbp, L=lp, C=cp)
        yb_s, sb_s, cb_s = _ssd(proj, ssm_conv_w[l], ssd_cb, ssd_par, ssd_gn, _pack_ssm_state(state_ssm[l]),
                                state_ssm_conv[l], row0=tp, nb=nbs, L=ls, C=cs_len)

        yc_p, sc_p = _ret(proj, inv, None, row0=0, nb=nbp, L=lp, C=cp, pos0=0)
        yc_s, sc_s = _ret(proj, inv, state_ret[l], row0=tp, nb=nbs, L=ls, C=cs_len, pos0=PAST_LEN)

        ya = jnp.concatenate([ya_p, ya_s], axis=0).astype(bf16)
        yb = jnp.concatenate([yb_p, yb_s], axis=0).astype(bf16)
        yc = jnp.concatenate([yc_p, yc_s], axis=0).astype(bf16)
        h = _merge(ya, yb, yc, w_branch_a[l].astype(bf16), w_branch_b[l].astype(bf16), w_branch_c[l].astype(bf16), proj)
        x = _matmul(h, w_out[l].astype(bf16), res=x, name="out_proj")

        u2 = _rmsnorm(x, norm_ffn[l], bf16)
        gate = _matmul(u2, w_ffn_gate[l].astype(bf16), tn=256, name="ffn_gate")
        up = _matmul(u2, w_ffn_up[l].astype(bf16), tn=256, name="ffn_up")
        fcb = ffn_conv_b[l].reshape(1, D_FF)
        act_p, cf_p = _ffn_act(gate, up, ffn_conv_w[l], fcb, None, row0=0, nb=nbp, L=lp, tn=256, out_dtype=bf16)
        act_s, cf_s = _ffn_act(gate, up, ffn_conv_w[l], fcb, state_ffn_conv[l], row0=tp, nb=nbs, L=ls, tn=D_FF,
                               out_dtype=f32)
        act = jnp.concatenate([act_p, act_s.astype(bf16)], axis=0)
        x = _matmul(act, w_ffn_down[l].astype(bf16), res=x, tk=D_FF // 2, name="ffn_down")

        for lst, val in zip(outs_p, (sa_p, ca_p, _unpack_ssm_state(sb_p), cb_p, sc_p, cf_p)):
            lst.append(val)
        for lst, val in zip(outs_s, (sa_s, ca_s, _unpack_ssm_state(sb_s), cb_s, sc_s, cf_s)):
            lst.append(val)

    y = _rmsnorm(x, norm_final, f32)
    y_prompt = y[:tp].reshape(nbp, lp, d)
    y_sample = y[tp:].reshape(nbs, ls, d)
    return (y_prompt, y_sample) + tuple(jnp.stack(v) for v in outs_p) + tuple(jnp.stack(v) for v in outs_s)
```

```python
import functools

import numpy as np
import jax
import jax.numpy as jnp
from jax import lax
from jax.experimental import pallas as pl
from jax.experimental.pallas import tpu as pltpu

f32 = jnp.float32
bf16 = jnp.bfloat16

D_MODEL = 4096
A_HEADS, A_DK, A_DV = 16, 128, 128
A_QKV = A_HEADS * (2 * A_DK + A_DV)
B_HEADS, B_HEADDIM, B_GROUPS, B_DSTATE = 32, 64, 4, 128
B_DINNER = B_HEADS * B_HEADDIM
B_REP = B_HEADS // B_GROUPS
B_PAIRS = B_HEADS // 2
B_CONV_DIM = B_DINNER + 2 * B_GROUPS * B_DSTATE
C_HEADS, C_DK, C_DV = 8, 256, 256
SHORT_CONV = 4
FFN_CONV = 3
CHUNK = 64
D_FF = 11008
ROPE_BASE = 10000.0
EPS = 1e-6
PAST_LEN = 16384
MIX_W = 2048

OFF_QKV = 0
OFF_AZ = 6144
OFF_BZ = 8192
OFF_BX = 10240
OFF_CQ = 12288
OFF_CK = 14336
OFF_CV = 16384
OFF_CG = 18432
OFF_GATES = 20480
OFF_BB = 32768
OFF_BC = 33280
OFF_SMALL = 33792
NP = 34304
LANE_AB, LANE_AA, LANE_BDT = 0, 16, 32

VMEM_LIMIT = 56 * 1024 * 1024
NEG_BIG = -1e30
SAMPLE_SEQS_PER_STEP = 2


def _cparams(*sem):
    return pltpu.CompilerParams(dimension_semantics=sem, vmem_limit_bytes=VMEM_LIMIT)


def _tile(n, pref, mult):
    t = min(pref, n)
    t -= t % mult
    while t >= mult:
        if n % t == 0:
            return t
        t -= mult
    return n


def _sigmoid(x):
    return 1.0 / (1.0 + jnp.exp(-x))


def _silu(x):
    return x * _sigmoid(x)


def _softplus(x):
    return jnp.maximum(x, 0.0) + jnp.log(1.0 + jnp.exp(-jnp.abs(x)))


def _dot(a, b):
    return jnp.dot(a.astype(bf16), b.astype(bf16), preferred_element_type=f32)


def _dot_nt(a, b):
    return lax.dot_general(a.astype(bf16), b.astype(bf16), (((1,), (1,)), ((), ())), preferred_element_type=f32)


def _dot_tn(a, b):
    return lax.dot_general(a.astype(bf16), b.astype(bf16), (((0,), (0,)), ((), ())), preferred_element_type=f32)


def _dot3(a, b):
    ah = a.astype(bf16).astype(f32)
    al = a - ah
    bh = b.astype(bf16).astype(f32)
    bl = b - bh
    if a.shape[1] == 64:
        a3 = jnp.concatenate([ah, al, ah], axis=1).astype(bf16)
        b3 = jnp.concatenate([bh, bh, bl], axis=0).astype(bf16)
        return jnp.dot(a3, b3, preferred_element_type=f32)
    d = functools.partial(jnp.dot, preferred_element_type=f32)
    ah, al, bh, bl = ah.astype(bf16), al.astype(bf16), bh.astype(bf16), bl.astype(bf16)
    return d(ah, bh) + (d(al, bh) + d(ah, bl))


def _cumsum_rows(x, n):
    row = lax.broadcasted_iota(jnp.int32, x.shape, 0)
    s = 1
    while s < n:
        x = x + jnp.where(row >= s, pltpu.roll(x, s, axis=0), 0.0)
        s *= 2
    return x


def _col_to_row(col, eye):
    return jnp.sum(jnp.where(eye, col, 0.0), axis=0, keepdims=True)


def _inv_unit_lower(lmats, ii, jj, n):
    eye = (ii == jj).astype(f32)
    blk = (ii >> 3) == (jj >> 3)
    ps = [jnp.where(blk, -l, 0.0) for l in lmats]
    ts = [eye + p for p in ps]
    p2 = [_dot3(p, p) for p in ps]
    ts = [t + _dot3(t, q) for t, q in zip(ts, p2)]
    p4 = [_dot3(q, q) for q in p2]
    ts = [t + _dot3(t, q) for t, q in zip(ts, p4)]
    s = 8
    while s < n:
        sh = s.bit_length() - 1
        mask = ((ii >> (sh + 1)) == (jj >> (sh + 1))) & ((ii >> sh) != (jj >> sh))
        mids = [_dot3(jnp.where(mask, l, 0.0), t) for l, t in zip(lmats, ts)]
        ts = [t - _dot3(t, m) for t, m in zip(ts, mids)]
        s *= 2
    return ts


def _alias_args(n_fixed, bufs):
    arrays, aliases = [], {}
    for out_idx, buf in enumerate(bufs):
        if buf is not None:
            aliases[n_fixed + len(arrays)] = out_idx
            arrays.append(buf)
    return arrays, [pl.BlockSpec(memory_space=pl.ANY)] * len(arrays), aliases


def _rmsnorm_kernel(x_ref, g_ref, o_ref):
    x = x_ref[...]
    y = x * lax.rsqrt(jnp.mean(x * x, axis=-1, keepdims=True) + EPS) * g_ref[...]
    o_ref[...] = y.astype(o_ref.dtype)


def _rmsnorm(x, gain, out_dtype):
    t, d = x.shape
    tm = _tile(t, 256, 16)
    return pl.pallas_call(
        _rmsnorm_kernel,
        grid=(t // tm,),
        in_specs=[pl.BlockSpec((tm, d), lambda i: (i, 0)), pl.BlockSpec((1, d), lambda i: (0, 0))],
        out_specs=pl.BlockSpec((tm, d), lambda i: (i, 0)),
        out_shape=jax.ShapeDtypeStruct((t, d), out_dtype),
        compiler_params=_cparams("parallel"),
        name="rmsnorm",
    )(x, gain.reshape(1, d))


def _mm_kernel(nk, has_res, *refs):
    if has_res:
        a_ref, b_ref, r_ref, o_ref = refs[:4]
    else:
        a_ref, b_ref, o_ref = refs[:3]
        r_ref = None
    if nk == 1:
        acc = jnp.dot(a_ref[...], b_ref[...].astype(bf16), preferred_element_type=f32)
        if has_res:
            acc = r_ref[...] + acc
        o_ref[...] = acc.astype(o_ref.dtype)
        return
    acc_ref = refs[-1]
    k = pl.program_id(2)

    @pl.when(k == 0)
    def _():
        acc_ref[...] = jnp.zeros_like(acc_ref)

    acc_ref[...] += jnp.dot(a_ref[...], b_ref[...].astype(bf16), preferred_element_type=f32)

    @pl.when(k == nk - 1)
    def _():
        acc = acc_ref[...]
        if has_res:
            acc = r_ref[...] + acc
        o_ref[...] = acc.astype(o_ref.dtype)


def _matmul(a, b, res=None, *, layer=None, tm=1024, tn=512, tk=None, out_dtype=f32, name="matmul"):
    m, kdim = a.shape
    n = b.shape[-1]
    tm = _tile(m, tm, 16)
    tn = _tile(n, tn, 128)
    tk = kdim if tk is None else tk
    nk = kdim // tk
    if layer is None:
        b_spec = pl.BlockSpec((tk, tn), lambda i, j, k: (k, j))
    else:
        b_spec = pl.BlockSpec((None, tk, tn), lambda i, j, k: (layer, k, j))
    in_specs = [pl.BlockSpec((tm, tk), lambda i, j, k: (i, k)), b_spec]
    args = [a, b]
    if res is not None:
        in_specs.append(pl.BlockSpec((tm, tn), lambda i, j, k: (i, j)))
        args.append(res)
    return pl.pallas_call(
        functools.partial(_mm_kernel, nk, res is not None),
        grid=(m // tm, n // tn, nk),
        in_specs=in_specs,
        out_specs=pl.BlockSpec((tm, tn), lambda i, j, k: (i, j)),
        out_shape=jax.ShapeDtypeStruct((m, n), out_dtype),
        scratch_shapes=[pltpu.VMEM((tm, tn), f32)] if nk > 1 else [],
        compiler_params=_cparams("parallel", "parallel", "arbitrary"),
        name=name,
    )(*args)


def _merge_kernel(ya_ref, yb_ref, yc_ref, wa_ref, wb_ref, wc_ref, ga_ref, gb_ref, gc_ref, o_ref):
    d = functools.partial(jnp.dot, preferred_element_type=f32)
    h = _sigmoid(ga_ref[...]) * d(ya_ref[...], wa_ref[...])
    h = h + _sigmoid(gb_ref[...]) * d(yb_ref[...], wb_ref[...])
    h = h + _sigmoid(gc_ref[...]) * d(yc_ref[...], wc_ref[...])
    o_ref[...] = h.astype(o_ref.dtype)


def _merge(ya, yb, yc, wa, wb, wc, proj):
    t, kd = ya.shape
    n = wa.shape[1]
    tm = _tile(t, 512, 16)
    tn = 512
    g0 = OFF_GATES // tn
    gstep = n // tn
    yspec = pl.BlockSpec((tm, kd), lambda i, j: (i, 0))
    wspec = pl.BlockSpec((kd, tn), lambda i, j: (0, j))
    gspecs = [pl.BlockSpec((tm, tn), functools.partial(lambda i, j, o: (i, o + j), o=g0 + r * gstep)) for r in range(3)]
    return pl.pallas_call(
        _merge_kernel,
        grid=(t // tm, n // tn),
        in_specs=[yspec, yspec, yspec, wspec, wspec, wspec] + gspecs,
        out_specs=pl.BlockSpec((tm, tn), lambda i, j: (i, j)),
        out_shape=jax.ShapeDtypeStruct((t, n), bf16),
        compiler_params=_cparams("parallel", "parallel"),
        name="merge",
    )(ya, yb, yc, wa, wb, wc, proj, proj, proj)


def _mixer_call(kernel_fn, name, proj_cols, consts, states_in, state_shapes, scratch, ybuf, prev_states, *,
                proj, layer, depth, ntok, row0, nb, L, C, NS):
    nch = L // C
    assert NS == 1 or nch == 1
    rows = NS * C
    rb0 = row0 // rows

    def rowmap(col):
        return lambda i, c: (rb0 + i * nch + c, col)

    def stmap(nd):
        return lambda i, c: (layer, i) + (0,) * nd

    in_specs = [pl.BlockSpec((rows, w), rowmap(off // w)) for w, off in proj_cols]
    args = [proj] * len(proj_cols)
    for arr, shp in consts:
        in_specs.append(pl.BlockSpec(shp, functools.partial(lambda i, c, nd: (0,) * nd, nd=len(shp))))
        args.append(arr)
    st_specs = [pl.BlockSpec((1, NS) + shp, stmap(len(shp))) for shp in state_shapes]
    for arr, spec in zip(states_in, st_specs):
        in_specs.append(spec)
        args.append(arr)
    al_arrays, al_specs, aliases = _alias_args(len(args), [ybuf] + list(prev_states))
    return pl.pallas_call(
        functools.partial(kernel_fn, C, NS, bool(states_in), len(al_arrays)),
        grid=(nb // NS, nch),
        in_specs=in_specs + al_specs,
        out_specs=[pl.BlockSpec((rows, MIX_W), rowmap(0))] + st_specs,
        out_shape=[jax.ShapeDtypeStruct((ntok, MIX_W), bf16)]
        + [jax.ShapeDtypeStruct((depth, nb) + shp, f32) for shp in state_shapes],
        scratch_shapes=scratch + [pltpu.VMEM((rows, MIX_W), f32)],
        input_output_aliases=aliases,
        compiler_params=_cparams("parallel", "arbitrary"),
        name=name + ("_state" if states_in else ""),
    )(*(args + al_arrays))


def _gdn_kernel(C, NS, has_state, n_alias, *refs):
    qkv_ref, z_ref, sm_ref, cw_ref, par_ref, gn_ref = refs[:6]
    n_in = 6
    if has_state:
        s0_ref, cs_ref = refs[6:8]
        n_in = 8
    y_ref, so_ref, cso_ref, ext_ref, yacc_ref = refs[n_in + n_alias:]
    c = pl.program_id(1)

    @pl.when(c == 0)
    def _():
        for s in range(NS):
            ext_ref[s, 0:8, :] = jnp.zeros((8, A_QKV), f32)
            if has_state:
                ext_ref[s, 5:8, :] = cs_ref[0, s]
        if has_state:
            so_ref[...] = s0_ref[...]
        else:
            so_ref[...] = jnp.zeros_like(so_ref)

    for s in range(NS):
        ext_ref[s, 8:8 + C, :] = qkv_ref[s * C:(s + 1) * C, :]

    def conv_silu(s, lo):
        acc = cw_ref[0:1, lo:lo + 128] * ext_ref[s, 5:5 + C, lo:lo + 128]
        for j in range(1, SHORT_CONV):
            acc = acc + cw_ref[j:j + 1, lo:lo + 128] * ext_ref[s, 5 + j:5 + j + C, lo:lo + 128]
        return _silu(acc)

    gates = []
    for s in range(NS):
        sm = sm_ref[s * C:(s + 1) * C, :]
        beta_all = _sigmoid(sm)
        g_all = -jnp.exp(par_ref[0:1, :]) * _softplus(sm + par_ref[1:2, :])
        gc_all = _cumsum_rows(g_all, C)
        gl_all = gc_all[C - 1:C, :]
        gates.append((beta_all, gc_all, jnp.exp(gc_all), jnp.exp(gl_all), jnp.exp(gl_all - gc_all)))

    ii = lax.broadcasted_iota(jnp.int32, (C, C), 0)
    jj = lax.broadcasted_iota(jnp.int32, (C, C), 1)
    eye = ii == jj
    tri = ii >= jj
    strict = ii > jj
    gain = gn_ref[...]

    def col(s, which, lane):
        return gates[s][which][:, lane:lane + 1]

    insts = [(s, h) for h in range(A_HEADS) for s in range(NS)]
    qs = [conv_silu(s, h * A_DK) for s, h in insts]
    qs = [q * lax.rsqrt(jnp.sum(q * q, axis=-1, keepdims=True) + EPS) * (A_DK ** -0.5) for q in qs]
    ks = [conv_silu(s, A_HEADS * A_DK + h * A_DK) for s, h in insts]
    ks = [k * lax.rsqrt(jnp.sum(k * k, axis=-1, keepdims=True) + EPS) for k in ks]
    vs = [conv_silu(s, 2 * A_HEADS * A_DK + h * A_DV) for s, h in insts]
    betas = [col(s, 0, LANE_AB + h) for s, h in insts]
    gcols = [col(s, 1, LANE_AA + h) for s, h in insts]
    egcs = [col(s, 2, LANE_AA + h) for s, h in insts]
    decays = [jnp.exp(jnp.where(tri, g - _col_to_row(g, eye), NEG_BIG)) for g in gcols]
    kbs = [k * b for k, b in zip(ks, betas)]
    lmats = [jnp.where(strict, _dot_nt(kb, k) * d, 0.0) for kb, k, d in zip(kbs, ks, decays)]
    rhss = [jnp.concatenate([v * b, kb * e], axis=-1) for v, b, kb, e in zip(vs, betas, kbs, egcs)]
    tinv = _inv_unit_lower(lmats, ii, jj, C)
    sols = [_dot3(t, r) for t, r in zip(tinv, rhss)]
    sts = [so_ref[0, s, h] for s, h in insts]
    vnews = [sol[:, :A_DV] - _dot(sol[:, A_DV:], st) for sol, st in zip(sols, sts)]
    qks = [jnp.where(tri, _dot_nt(q, k) * d, 0.0) for q, k, d in zip(qs, ks, decays)]
    outs = [_dot(q * e, st) + _dot(qk, vn) for q, e, st, qk, vn in zip(qs, egcs, sts, qks, vnews)]
    for (s, h), k, st, vn in zip(insts, ks, sts, vnews):
        so_ref[0, s, h] = st * col(s, 3, LANE_AA + h) + _dot_tn(k * col(s, 4, LANE_AA + h), vn)
    for (s, h), o in zip(insts, outs):
        o = o * lax.rsqrt(jnp.mean(o * o, axis=-1, keepdims=True) + EPS) * gain
        yacc_ref[s * C:(s + 1) * C, h * A_DV:(h + 1) * A_DV] = o * _silu(
            z_ref[s * C:(s + 1) * C, h * A_DV:(h + 1) * A_DV])
    y_ref[...] = yacc_ref[...].astype(y_ref.dtype)

    for s in range(NS):
        cso_ref[0, s] = ext_ref[s, C + 5:C + 8, :]
        ext_ref[s, 0:8, :] = ext_ref[s, C:C + 8, :]


def _gdn(proj, cw, par, gn, states_in, ybuf, prev_states, **kw):
    return _mixer_call(
        _gdn_kernel, "gdn",
        [(A_QKV, OFF_QKV), (MIX_W, OFF_AZ), (128, OFF_SMALL)],
        [(cw, (SHORT_CONV, A_QKV)), (par, (2, 128)), (gn, (1, A_DV))],
        states_in, [(A_HEADS, A_DK, A_DV), (SHORT_CONV - 1, A_QKV)],
        [pltpu.VMEM((kw["NS"], kw["C"] + 8, A_QKV), f32)], ybuf, prev_states, proj=proj, **kw)


def _ssd_kernel(C, NS, has_state, n_alias, *refs):
    z_ref, x_ref, bb_ref, bc_ref, sm_ref, cw_ref, cb_ref, par_ref, gn_ref = refs[:9]
    n_in = 9
    if has_state:
        s0_ref, cs_ref = refs[9:11]
        n_in = 11
    y_ref, so_ref, cso_ref, ext_ref, yacc_ref = refs[n_in + n_alias:]
    c = pl.program_id(1)
    ng = B_GROUPS * B_DSTATE

    @pl.when(c == 0)
    def _():
        for s in range(NS):
            ext_ref[s, 0:8, :] = jnp.zeros((8, B_CONV_DIM), f32)
            if has_state:
                ext_ref[s, 5:8, :] = cs_ref[0, s]
        if has_state:
            so_ref[...] = s0_ref[...]
        else:
            so_ref[...] = jnp.zeros_like(so_ref)

    ii = lax.broadcasted_iota(jnp.int32, (C, C), 0)
    jj = lax.broadcasted_iota(jnp.int32, (C, C), 1)
    eye = ii == jj
    tri = ii >= jj
    first = lax.broadcasted_iota(jnp.int32, (C, 128), 1) < B_HEADDIM
    first_lane = lax.broadcasted_iota(jnp.int32, (1, 128), 1) < B_HEADDIM
    first_row = lax.broadcasted_iota(jnp.int32, (2 * B_HEADDIM, 1), 0) < B_HEADDIM
    pairs_per_group = B_REP // 2

    for s in range(NS):
        r0 = s * C
        ext_ref[s, 8:8 + C, 0:B_DINNER] = x_ref[r0:r0 + C, :]
        ext_ref[s, 8:8 + C, B_DINNER:B_DINNER + ng] = bb_ref[r0:r0 + C, :]
        ext_ref[s, 8:8 + C, B_DINNER + ng:B_CONV_DIM] = bc_ref[r0:r0 + C, :]

    def conv_silu(s, lo):
        acc = cw_ref[0:1, lo:lo + 128] * ext_ref[s, 5:5 + C, lo:lo + 128]
        for j in range(1, SHORT_CONV):
            acc = acc + cw_ref[j:j + 1, lo:lo + 128] * ext_ref[s, 5 + j:5 + j + C, lo:lo + 128]
        return _silu(acc + cb_ref[0:1, lo:lo + 128])

    def pick(arr, l0, mask):
        return jnp.where(mask, arr[:, l0:l0 + 1], arr[:, l0 + 1:l0 + 2])

    gates = []
    for s in range(NS):
        sm = sm_ref[s * C:(s + 1) * C, :]
        dt_all = _softplus(sm + par_ref[1:2, :])
        gc_all = _cumsum_rows(dt_all * (-jnp.exp(par_ref[0:1, :])), C)
        gl_all = gc_all[C - 1:C, :]
        gates.append((dt_all, gc_all, jnp.exp(gc_all), jnp.exp(gl_all), jnp.exp(gl_all - gc_all)))
    d_all = par_ref[2:3, :]

    sg = [(s, g) for s in range(NS) for g in range(B_GROUPS)]
    bgs = {k: conv_silu(k[0], B_DINNER + k[1] * B_DSTATE) for k in sg}
    cgs = {k: conv_silu(k[0], B_DINNER + ng + k[1] * B_DSTATE) for k in sg}
    scores = {k: _dot_nt(cgs[k], bgs[k]) for k in sg}
    insts = [(s, p) for s in range(NS) for p in range(B_PAIRS)]
    lanes = [LANE_BDT + 2 * p for _, p in insts]
    grp = [(s, p // pairs_per_group) for s, p in insts]
    xps = [conv_silu(s, p * 128) for s, p in insts]
    vs = [xp * pick(gates[s][0], l0, first) for xp, (s, _), l0 in zip(xps, insts, lanes)]
    vbs = [v.astype(bf16) for v in vs]
    sds = []
    for (s, _), l0, k, vb in zip(insts, lanes, grp, vbs):
        pair = []
        for l in (l0, l0 + 1):
            gcol = gates[s][1][:, l:l + 1]
            decay = jnp.exp(jnp.where(tri, gcol - _col_to_row(gcol, eye), NEG_BIG))
            pair.append(jnp.dot((scores[k] * decay).astype(bf16), vb, preferred_element_type=f32))
        sds.append(pair)
    sts = [so_ref[0, s, p] for s, p in insts]
    outs = [jnp.where(first, sd[0], sd[1]) + _dot_nt(cgs[k], st) * pick(gates[s][2], l0, first)
            for sd, k, st, (s, _), l0 in zip(sds, grp, sts, insts, lanes)]
    for (s, p), l0, k, st, v in zip(insts, lanes, grp, sts, vs):
        so_ref[0, s, p] = st * pick(gates[s][3], l0, first_row) + _dot_tn(v * pick(gates[s][4], l0, first), bgs[k])
    yvs = [(o + pick(d_all, l0, first_lane) * xp) * _silu(z_ref[s * C:(s + 1) * C, p * 128:(p + 1) * 128])
           for o, l0, xp, (s, p) in zip(outs, lanes, xps, insts)]
    ssq = {k: jnp.zeros((C, 1), f32) for k in sg}
    for k, yv in zip(grp, yvs):
        ssq[k] = ssq[k] + jnp.sum(yv * yv, axis=-1, keepdims=True)
    scale = {k: lax.rsqrt(v * (1.0 / (B_DINNER // B_GROUPS)) + EPS) for k, v in ssq.items()}
    for (s, p), k, yv in zip(insts, grp, yvs):
        yacc_ref[s * C:(s + 1) * C, p * 128:(p + 1) * 128] = yv * scale[k] * gn_ref[0:1, p * 128:(p + 1) * 128]
    y_ref[...] = yacc_ref[...].astype(y_ref.dtype)

    for s in range(NS):
        cso_ref[0, s] = ext_ref[s, C + 5:C + 8, :]
        ext_ref[s, 0:8, :] = ext_ref[s, C:C + 8, :]


def _ssd(proj, cw, cb, par, gn, states_in, ybuf, prev_states, **kw):
    ng = B_GROUPS * B_DSTATE
    return _mixer_call(
        _ssd_kernel, "ssd",
        [(MIX_W, OFF_BZ), (MIX_W, OFF_BX), (ng, OFF_BB), (ng, OFF_BC), (128, OFF_SMALL)],
        [(cw, (SHORT_CONV, B_CONV_DIM)), (cb, (1, B_CONV_DIM)), (par, (3, 128)), (gn, (1, B_DINNER))],
        states_in, [(B_PAIRS, 2 * B_HEADDIM, B_DSTATE), (SHORT_CONV - 1, B_CONV_DIM)],
        [pltpu.VMEM((kw["NS"], kw["C"] + 8, B_CONV_DIM), f32)], ybuf, prev_states, proj=proj, **kw)


def _pack_ssm_state(s):
    lead = s.shape[:-3]
    return jnp.swapaxes(s, -1, -2).reshape(lead + (B_PAIRS, 2 * B_HEADDIM, B_DSTATE))


def _unpack_ssm_state(s):
    lead = s.shape[:-3]
    return jnp.swapaxes(s.reshape(lead + (B_HEADS, B_HEADDIM, B_DSTATE)), -1, -2)


_LOG_GAMMA = [float(np.log(np.float32(1.0) - np.float32(2.0) ** np.float32(-5.0 - h))) for h in range(C_HEADS)]


def _ret_kernel(pos0, C, NS, has_state, n_alias, *refs):
    q_ref, k_ref, v_ref, g_ref, inv_ref = refs[:5]
    n_in = 5
    if has_state:
        s0_ref = refs[5]
        n_in = 6
    y_ref, so_ref, yacc_ref = refs[n_in + n_alias:]
    c = pl.program_id(1)

    @pl.when(c == 0)
    def _():
        if has_state:
            so_ref[...] = s0_ref[...]
        else:
            so_ref[...] = jnp.zeros_like(so_ref)

    half = C_DK // 2
    pos = (pos0 + c * C + lax.broadcasted_iota(jnp.int32, (C, half), 0)).astype(f32)
    ang = pos * inv_ref[...]
    cos = jnp.cos(ang)
    sin = jnp.sin(ang)

    ii = lax.broadcasted_iota(jnp.int32, (C, C), 0)
    jj = lax.broadcasted_iota(jnp.int32, (C, C), 1)
    tri = ii >= jj
    dij = (ii - jj).astype(f32)
    ipos = (lax.broadcasted_iota(jnp.int32, (C, 1), 0) + 1).astype(f32)

    for s in range(NS):
        r0 = s * C

        def rot(ref, h, r0=r0):
            t1 = ref[r0:r0 + C, h * C_DK:h * C_DK + half]
            t2 = ref[r0:r0 + C, h * C_DK + half:(h + 1) * C_DK]
            return jnp.concatenate([t1 * cos - t2 * sin, t1 * sin + t2 * cos], axis=-1)

        for h in range(C_HEADS):
            lg = _LOG_GAMMA[h]
            q = rot(q_ref, h)
            k = rot(k_ref, h) * (C_DK ** -0.5)
            v = v_ref[r0:r0 + C, h * C_DV:(h + 1) * C_DV]
            decay = jnp.exp(jnp.where(tri, dij * lg, NEG_BIG))
            egc = jnp.exp(ipos * lg)
            edl = jnp.exp((float(C) - ipos) * lg)
            egl = float(np.exp(np.float32(C * lg)))
            st = so_ref[0, s, h]
            o = _dot(_dot_nt(q, k) * decay, v) + _dot(q, st) * egc
            so_ref[0, s, h] = st * egl + _dot_tn(k, v * edl)
            o = o * lax.rsqrt(jnp.mean(o * o, axis=-1, keepdims=True) + EPS)
            yacc_ref[r0:r0 + C, h * C_DV:(h + 1) * C_DV] = o * _silu(g_ref[r0:r0 + C, h * C_DV:(h + 1) * C_DV])
    y_ref[...] = yacc_ref[...].astype(y_ref.dtype)


def _ret(proj, inv, states_in, ybuf, prev_states, *, pos0, **kw):
    return _mixer_call(
        functools.partial(_ret_kernel, pos0), "ret",
        [(MIX_W, OFF_CQ), (MIX_W, OFF_CK), (MIX_W, OFF_CV), (MIX_W, OFF_CG)],
        [(inv, (1, C_DK // 2))],
        states_in, [(C_HEADS, C_DK, C_DV)], [], ybuf, prev_states, proj=proj, **kw)


def _ffn_act_kernel(L, NS, has_state, n_alias, *refs):
    g_ref, u_ref, w_ref, b_ref = refs[:4]
    n_in = 4
    if has_state:
        cs_ref = refs[4]
        n_in = 5
    a_ref, cso_ref, ext_ref, acc_ref = refs[n_in + n_alias:]
    n = ext_ref.shape[2]
    for s in range(NS):
        r0 = s * L
        ext_ref[s, 0:8, :] = jnp.zeros((8, n), f32)
        if has_state:
            ext_ref[s, 6:8, :] = cs_ref[0, s]
        ext_ref[s, 8:8 + L, :] = g_ref[r0:r0 + L, :]
        acc = w_ref[0:1, :] * ext_ref[s, 6:6 + L, :]
        for j in range(1, FFN_CONV):
            acc = acc + w_ref[j:j + 1, :] * ext_ref[s, 6 + j:6 + j + L, :]
        acc = acc + b_ref[...]
        acc_ref[r0:r0 + L, :] = _silu(acc) * u_ref[r0:r0 + L, :]
        cso_ref[0, s] = ext_ref[s, L + 6:L + 8, :]
    a_ref[...] = acc_ref[...].astype(a_ref.dtype)


def _ffn_act(gate, up, w, b, cs, abuf, prev_state, *, layer, depth, ntok, row0, nb, L, NS, tn):
    has_state = cs is not None
    rows = NS * L
    rb0 = row0 // rows
    nf = gate.shape[1]
    in_specs = [
        pl.BlockSpec((rows, tn), lambda i, j: (rb0 + i, j)),
        pl.BlockSpec((rows, tn), lambda i, j: (rb0 + i, j)),
        pl.BlockSpec((FFN_CONV, tn), lambda i, j: (0, j)),
        pl.BlockSpec((1, tn), lambda i, j: (0, j)),
    ]
    args = [gate, up, w, b]
    cs_spec = pl.BlockSpec((1, NS, FFN_CONV - 1, tn), lambda i, j: (layer, i, 0, j))
    if has_state:
        in_specs.append(cs_spec)
        args.append(cs)
    al_arrays, al_specs, aliases = _alias_args(len(args), [abuf, prev_state])
    return pl.pallas_call(
        functools.partial(_ffn_act_kernel, L, NS, has_state, len(al_arrays)),
        grid=(nb // NS, nf // tn),
        in_specs=in_specs + al_specs,
        out_specs=[pl.BlockSpec((rows, tn), lambda i, j: (rb0 + i, j)), cs_spec],
        out_shape=[jax.ShapeDtypeStruct((ntok, nf), bf16),
                   jax.ShapeDtypeStruct((depth, nb, FFN_CONV - 1, nf), f32)],
        scratch_shapes=[pltpu.VMEM((NS, L + 8, tn), f32), pltpu.VMEM((rows, tn), f32)],
        input_output_aliases=aliases,
        compiler_params=_cparams("parallel", "parallel"),
        name="ffn_act_state" if has_state else "ffn_act",
    )(*(args + al_arrays))


def _reorder_w_in(w):
    o_ab = A_QKV + A_HEADS * A_DV
    o_bz = o_ab + 2 * A_HEADS
    o_bx = o_bz + B_DINNER
    o_bbc = o_bx + B_DINNER
    o_dt = o_bx + B_CONV_DIM
    o_cq = o_dt + B_HEADS
    total = w.shape[1]
    parts = [w[:, 0:o_ab], w[:, o_bz:o_bbc], w[:, o_cq:total], w[:, o_bbc:o_dt], w[:, o_ab:o_bz], w[:, o_dt:o_cq]]
    used = sum(p.shape[1] for p in parts)
    parts.append(jnp.zeros((w.shape[0], NP - used), w.dtype))
    return jnp.concatenate(parts, axis=1).astype(bf16)


def _lane_row(vals, lane0):
    return jnp.zeros((128,), f32).at[lane0:lane0 + vals.shape[0]].set(vals.astype(f32))


def kernel(x_prompt, x_sample, state_gdn, state_gdn_conv, state_ssm, state_ssm_conv, state_ret, state_ffn_conv, norm_mix, w_in, gdn_conv_w, gdn_a_log, gdn_dt_bias, gdn_norm, ssm_conv_w, ssm_conv_b, ssm_a_log, ssm_dt_bias, ssm_d, ssm_norm, w_branch_a, w_branch_b, w_branch_c, w_out, norm_ffn, w_ffn_gate, w_ffn_up, ffn_conv_w, ffn_conv_b, w_ffn_down, norm_final):
    nbp, lp, d = x_prompt.shape
    nbs, ls, _ = x_sample.shape
    depth = w_in.shape[0]
    tp = nbp * lp
    ts = nbs * ls
    ntok = tp + ts
    cp = CHUNK if lp % CHUNK == 0 else lp
    cs_len = CHUNK if ls % CHUNK == 0 else ls
    ns = SAMPLE_SEQS_PER_STEP

    x = jnp.concatenate([x_prompt.reshape(tp, d), x_sample.reshape(ts, d)], axis=0)
    half = C_DK // 2
    inv = (ROPE_BASE ** (-jnp.arange(half, dtype=f32) / half)).reshape(1, half)
    ssm_in = _pack_ssm_state(state_ssm)

    gdn_p = gdn_conv_p = ssm_p = ssm_conv_p = ret_p = ffn_conv_p = None
    gdn_s = gdn_conv_s = ssm_s = ssm_conv_s = ret_s = ffn_conv_s = None
    for l in range(depth):
        kp = dict(layer=l, depth=depth, ntok=ntok, row0=0, nb=nbp, L=lp, C=cp, NS=1)
        ks = dict(layer=l, depth=depth, ntok=ntok, row0=tp, nb=nbs, L=ls, C=cs_len, NS=ns)
        u = _rmsnorm(x, norm_mix[l], bf16)
        proj = _matmul(u, _reorder_w_in(w_in[l]), name="in_proj")

        gdn_par = jnp.stack([_lane_row(gdn_a_log[l], LANE_AA), _lane_row(gdn_dt_bias[l], LANE_AA)])
        gdn_gn = gdn_norm[l].reshape(1, A_DV)
        ya, gdn_p, gdn_conv_p = _gdn(proj, gdn_conv_w[l], gdn_par, gdn_gn, [], None, [gdn_p, gdn_conv_p], **kp)
        ya, gdn_s, gdn_conv_s = _gdn(proj, gdn_conv_w[l], gdn_par, gdn_gn, [state_gdn, state_gdn_conv], ya,
                                     [gdn_s, gdn_conv_s], **ks)

        ssd_par = jnp.stack([_lane_row(ssm_a_log[l], LANE_BDT), _lane_row(ssm_dt_bias[l], LANE_BDT),
                             _lane_row(ssm_d[l], LANE_BDT)])
        ssd_cb = ssm_conv_b[l].reshape(1, B_CONV_DIM)
        ssd_gn = ssm_norm[l].reshape(1, B_DINNER)
        yb, ssm_p, ssm_conv_p = _ssd(proj, ssm_conv_w[l], ssd_cb, ssd_par, ssd_gn, [], None, [ssm_p, ssm_conv_p], **kp)
        yb, ssm_s, ssm_conv_s = _ssd(proj, ssm_conv_w[l], ssd_cb, ssd_par, ssd_gn, [ssm_in, state_ssm_conv], yb,
                                     [ssm_s, ssm_conv_s], **ks)

        yc, ret_p = _ret(proj, inv, [], None, [ret_p], pos0=0, **kp)
        yc, ret_s = _ret(proj, inv, [state_ret], yc, [ret_s], pos0=PAST_LEN, **ks)

        h = _merge(ya, yb, yc, w_branch_a[l].astype(bf16), w_branch_b[l].astype(bf16), w_branch_c[l].astype(bf16), proj)
        x = _matmul(h, w_out, res=x, layer=l, name="out_proj")

        u2 = _rmsnorm(x, norm_ffn[l], bf16)
        gate = _matmul(u2, w_ffn_gate, layer=l, tn=256, name="ffn_gate")
        up = _matmul(u2, w_ffn_up, layer=l, tn=256, name="ffn_up")
        fcb = ffn_conv_b[l].reshape(1, D_FF)
        kf = dict(layer=l, depth=depth, ntok=ntok)
        act, ffn_conv_p = _ffn_act(gate, up, ffn_conv_w[l], fcb, None, None, ffn_conv_p,
                                   row0=0, nb=nbp, L=lp, NS=1, tn=256, **kf)
        act, ffn_conv_s = _ffn_act(gate, up, ffn_conv_w[l], fcb, state_ffn_conv, act, ffn_conv_s,
                                   row0=tp, nb=nbs, L=ls, NS=ns, tn=D_FF, **kf)
        x = _matmul(act, w_ffn_down[l].astype(bf16), res=x, tk=D_FF // 2, name="ffn_down")

    y = _rmsnorm(x, norm_final, f32)
    y_prompt = y[:tp].reshape(nbp, lp, d)
    y_sample = y[tp:].reshape(nbs, ls, d)
    return (y_prompt, y_sample,
            gdn_p, gdn_conv_p, _unpack_ssm_state(ssm_p), ssm_conv_p, ret_p, ffn_conv_p,
            gdn_s, gdn_conv_s, _unpack_ssm_state(ssm_s), ssm_conv_s, ret_s, ffn_conv_s)
```

```python
import functools

import numpy as np
import jax
import jax.numpy as jnp
from jax import lax
from jax.experimental import pallas as pl
from jax.experimental.pallas import tpu as pltpu

f32 = jnp.float32
bf16 = jnp.bfloat16

D_MODEL = 4096
A_HEADS, A_DK, A_DV = 16, 128, 128
A_QKV = A_HEADS * (2 * A_DK + A_DV)
B_HEADS, B_HEADDIM, B_GROUPS, B_DSTATE = 32, 64, 4, 128
B_DINNER = B_HEADS * B_HEADDIM
B_REP = B_HEADS // B_GROUPS
B_PAIRS = B_HEADS // 2
B_CONV_DIM = B_DINNER + 2 * B_GROUPS * B_DSTATE
C_HEADS, C_DK, C_DV = 8, 256, 256
SHORT_CONV = 4
FFN_CONV = 3
CHUNK = 64
D_FF = 11008
ROPE_BASE = 10000.0
EPS = 1e-6
PAST_LEN = 16384
MIX_W = 2048

OFF_QKV = 0
OFF_AZ = 6144
OFF_BZ = 8192
OFF_BX = 10240
OFF_CQ = 12288
OFF_CK = 14336
OFF_CV = 16384
OFF_CG = 18432
OFF_GATES = 20480
OFF_BB = 32768
OFF_BC = 33280
OFF_SMALL = 33792
NP = 34304
LANE_AB, LANE_AA, LANE_BDT = 0, 16, 32

VMEM_LIMIT = 56 * 1024 * 1024
NEG_BIG = -1e30
SAMPLE_SEQS_PER_STEP = 2


def _cparams(*sem):
    return pltpu.CompilerParams(dimension_semantics=sem, vmem_limit_bytes=VMEM_LIMIT)


def _tile(n, pref, mult):
    t = min(pref, n)
    t -= t % mult
    while t >= mult:
        if n % t == 0:
            return t
        t -= mult
    return n


def _sigmoid(x):
    return 1.0 / (1.0 + jnp.exp(-x))


def _silu(x):
    return x * _sigmoid(x)


def _softplus(x):
    return jnp.maximum(x, 0.0) + jnp.log(1.0 + jnp.exp(-jnp.abs(x)))


def _dot(a, b):
    return jnp.dot(a.astype(bf16), b.astype(bf16), preferred_element_type=f32)


def _dot_nt(a, b):
    return lax.dot_general(a.astype(bf16), b.astype(bf16), (((1,), (1,)), ((), ())), preferred_element_type=f32)


def _dot_tn(a, b):
    return lax.dot_general(a.astype(bf16), b.astype(bf16), (((0,), (0,)), ((), ())), preferred_element_type=f32)


def _split(a):
    hi = lax.bitcast_convert_type(lax.bitcast_convert_type(a, jnp.int32) & jnp.int32(-65536), f32)
    return hi, a - hi


def _prep_l(sa):
    hi, lo = sa
    if hi.shape[1] == 64:
        return (jnp.concatenate([hi, lo, hi], axis=1).astype(bf16),)
    return hi.astype(bf16), lo.astype(bf16)


def _prep_r(sb):
    hi, lo = sb
    if hi.shape[0] == 64:
        return (jnp.concatenate([hi, hi, lo], axis=0).astype(bf16),)
    return hi.astype(bf16), lo.astype(bf16)


def _mm3(l, r):
    d = functools.partial(jnp.dot, preferred_element_type=f32)
    if len(l) == 1:
        return d(l[0], r[0])
    return d(l[0], r[0]) + (d(l[1], r[0]) + d(l[0], r[1]))


def _cumsum_rows(x, n):
    row = lax.broadcasted_iota(jnp.int32, x.shape, 0)
    s = 1
    while s < n:
        x = x + jnp.where(row >= s, pltpu.roll(x, s, axis=0), 0.0)
        s *= 2
    return x


def _col_to_row(col, eye):
    return jnp.sum(jnp.where(eye, col, 0.0), axis=0, keepdims=True)


def _inv_unit_lower(lmats, ii, jj, n):
    eye = (ii == jj).astype(f32)
    blk = (ii >> 3) == (jj >> 3)
    ps = [jnp.where(blk, -l, 0.0) for l in lmats]
    ts = [eye + p for p in ps]
    sp = [_split(p) for p in ps]
    p2 = [_mm3(_prep_l(s), _prep_r(s)) for s in sp]
    sp2 = [_split(q) for q in p2]
    p2r = [_prep_r(s) for s in sp2]
    ts = [t + _mm3(_prep_l(_split(t)), r) for t, r in zip(ts, p2r)]
    p4 = [_mm3(_prep_l(s), r) for s, r in zip(sp2, p2r)]
    ts = [t + _mm3(_prep_l(_split(t)), _prep_r(_split(q))) for t, q in zip(ts, p4)]
    s = 8
    while s < n:
        sh = s.bit_length() - 1
        mask = ((ii >> (sh + 1)) == (jj >> (sh + 1))) & ((ii >> sh) != (jj >> sh))
        st = [_split(t) for t in ts]
        mids = [_mm3(_prep_l(_split(jnp.where(mask, l, 0.0))), _prep_r(s_)) for l, s_ in zip(lmats, st)]
        ts = [t - _mm3(_prep_l(s_), _prep_r(_split(m))) for t, s_, m in zip(ts, st, mids)]
        s *= 2
    return ts


def _alias_args(n_fixed, bufs):
    arrays, aliases = [], {}
    for out_idx, buf in enumerate(bufs):
        if buf is not None:
            aliases[n_fixed + len(arrays)] = out_idx
            arrays.append(buf)
    return arrays, [pl.BlockSpec(memory_space=pl.ANY)] * len(arrays), aliases


def _rmsnorm_kernel(x_ref, g_ref, o_ref):
    x = x_ref[...]
    y = x * lax.rsqrt(jnp.mean(x * x, axis=-1, keepdims=True) + EPS) * g_ref[...]
    o_ref[...] = y.astype(o_ref.dtype)


def _rmsnorm(x, gain, out_dtype, row0=0, nrows=None):
    d = x.shape[1]
    t = x.shape[0] if nrows is None else nrows
    tm = _tile(int(np.gcd(t, row0)) if row0 else t, 256, 8)
    rb0 = row0 // tm
    return pl.pallas_call(
        _rmsnorm_kernel,
        grid=(t // tm,),
        in_specs=[pl.BlockSpec((tm, d), lambda i: (rb0 + i, 0)), pl.BlockSpec((1, d), lambda i: (0, 0))],
        out_specs=pl.BlockSpec((tm, d), lambda i: (i, 0)),
        out_shape=jax.ShapeDtypeStruct((t, d), out_dtype),
        compiler_params=_cparams("parallel"),
        name="rmsnorm",
    )(x, gain.reshape(1, d))


def _mm_kernel(nk, has_res, n_alias, *refs):
    a_ref, b_ref = refs[:2]
    r_ref = refs[2] if has_res else None
    o_ref = refs[2 + int(has_res) + n_alias]
    if nk == 1:
        acc = jnp.dot(a_ref[...], b_ref[...].astype(bf16), preferred_element_type=f32)
        if has_res:
            acc = r_ref[...] + acc
        o_ref[...] = acc.astype(o_ref.dtype)
        return
    acc_ref = refs[-1]
    k = pl.program_id(2)

    @pl.when(k == 0)
    def _():
        acc_ref[...] = jnp.zeros_like(acc_ref)

    acc_ref[...] += jnp.dot(a_ref[...], b_ref[...].astype(bf16), preferred_element_type=f32)

    @pl.when(k == nk - 1)
    def _():
        acc = acc_ref[...]
        if has_res:
            acc = r_ref[...] + acc
        o_ref[...] = acc.astype(o_ref.dtype)


def _matmul(a, b, res=None, *, layer=None, tm=1024, tn=512, tk=None, out_dtype=f32, name="matmul",
            ncols=None, out_buf=None, out_cols=None, col0=0):
    m, kdim = a.shape
    n = b.shape[-1] if ncols is None else ncols
    out_cols = n if out_cols is None else out_cols
    tm = _tile(m, tm, 16)
    tn = _tile(n, tn, 128)
    assert col0 % tn == 0
    cb0 = col0 // tn
    tk = kdim if tk is None else tk
    nk = kdim // tk
    if layer is None:
        b_spec = pl.BlockSpec((tk, tn), lambda i, j, k: (k, j))
    else:
        b_spec = pl.BlockSpec((None, tk, tn), lambda i, j, k: (layer, k, j))
    in_specs = [pl.BlockSpec((tm, tk), lambda i, j, k: (i, k)), b_spec]
    args = [a, b]
    if res is not None:
        in_specs.append(pl.BlockSpec((tm, tn), lambda i, j, k: (i, j)))
        args.append(res)
    al_arrays, al_specs, aliases = _alias_args(len(args), [out_buf])
    return pl.pallas_call(
        functools.partial(_mm_kernel, nk, res is not None, len(al_arrays)),
        grid=(m // tm, n // tn, nk),
        in_specs=in_specs + al_specs,
        out_specs=pl.BlockSpec((tm, tn), lambda i, j, k: (i, cb0 + j)),
        out_shape=jax.ShapeDtypeStruct((m, out_cols), out_dtype),
        scratch_shapes=[pltpu.VMEM((tm, tn), f32)] if nk > 1 else [],
        input_output_aliases=aliases,
        compiler_params=_cparams("parallel", "parallel", "arbitrary"),
        name=name,
    )(*(args + al_arrays))


IN_PROJ_TN = 512


def _in_proj_kernel(n_main, offs_ref, u_ref, wt_ref, ws_ref, o_ref):
    j = pl.program_id(1)

    @pl.when(j < n_main)
    def _():
        o_ref[...] = _dot_nt(u_ref[...], wt_ref[0])

    @pl.when(j == n_main)
    def _():
        o_ref[...] = _dot_nt(u_ref[...], ws_ref[...])


def _in_proj(u, w_in_t, layer):
    o_ab = A_QKV + A_HEADS * A_DV
    o_bz = o_ab + 2 * A_HEADS
    o_bbc = o_bz + 2 * B_DINNER
    o_dt = o_bbc + 2 * B_GROUPS * B_DSTATE
    o_cq = o_dt + B_HEADS
    total = w_in_t.shape[1]
    tn = IN_PROJ_TN
    runs = [(OFF_QKV, 0, o_ab), (OFF_BZ, o_bz, o_bbc - o_bz), (OFF_CQ, o_cq, total - o_cq), (OFF_BB, o_bbc, o_dt - o_bbc)]
    offs = []
    for col0, src, width in runs:
        assert col0 == len(offs) * tn and width % tn == 0 and src % 32 == 0
        offs += [src + t * tn for t in range(width // tn)]
    assert len(offs) * tn == OFF_SMALL and NP == OFF_SMALL + tn
    n_main = len(offs)
    offs.append(0)
    m, kdim = u.shape
    tm = _tile(m, 1024, 16)
    small = jnp.concatenate([w_in_t[layer, o_ab:o_bz, :], w_in_t[layer, o_dt:o_cq, :],
                             jnp.zeros((tn - (o_bz - o_ab) - (o_cq - o_dt), kdim), w_in_t.dtype)], axis=0)
    grid_spec = pltpu.PrefetchScalarGridSpec(
        num_scalar_prefetch=1, grid=(m // tm, n_main + 1),
        in_specs=[pl.BlockSpec((pl.Element(tm), pl.Element(kdim)), lambda i, j, o: (i * tm, 0)),
                  pl.BlockSpec((pl.Element(1), pl.Element(tn), pl.Element(kdim)),
                               lambda i, j, o: (layer, pl.multiple_of(o[j], 32), 0)),
                  pl.BlockSpec((pl.Element(tn), pl.Element(kdim)), lambda i, j, o: (0, 0))],
        out_specs=pl.BlockSpec((tm, tn), lambda i, j, o: (i, j)))
    return pl.pallas_call(
        functools.partial(_in_proj_kernel, n_main), grid_spec=grid_spec,
        out_shape=jax.ShapeDtypeStruct((m, NP), f32),
        compiler_params=_cparams("parallel", "arbitrary"),
        name="in_proj",
    )(jnp.asarray(offs, jnp.int32), u, w_in_t, small)


def _merge_kernel(ya_ref, yb_ref, yc_ref, wa_ref, wb_ref, wc_ref, ga_ref, gb_ref, gc_ref, o_ref):
    d = functools.partial(jnp.dot, preferred_element_type=f32)
    h = _sigmoid(ga_ref[...]) * d(ya_ref[...], wa_ref[...])
    h = h + _sigmoid(gb_ref[...]) * d(yb_ref[...], wb_ref[...])
    h = h + _sigmoid(gc_ref[...]) * d(yc_ref[...], wc_ref[...])
    o_ref[...] = h.astype(o_ref.dtype)


def _merge(ya, yb, yc, wa, wb, wc, proj, layer):
    t, kd = ya.shape
    n = wa.shape[-1]
    tm = _tile(t, 512, 16)
    tn = 512
    g0 = OFF_GATES // tn
    gstep = n // tn
    yspec = pl.BlockSpec((tm, kd), lambda i, j: (i, 0))
    wspec = pl.BlockSpec((None, kd, tn), lambda i, j: (layer, 0, j))
    gspecs = [pl.BlockSpec((tm, tn), functools.partial(lambda i, j, o: (i, o + j), o=g0 + r * gstep)) for r in range(3)]
    return pl.pallas_call(
        _merge_kernel,
        grid=(t // tm, n // tn),
        in_specs=[yspec, yspec, yspec, wspec, wspec, wspec] + gspecs,
        out_specs=pl.BlockSpec((tm, tn), lambda i, j: (i, j)),
        out_shape=jax.ShapeDtypeStruct((t, n), bf16),
        compiler_params=_cparams("parallel", "parallel"),
        name="merge",
    )(ya, yb, yc, wa, wb, wc, proj, proj, proj)


def _mixer_call(kernel_fn, name, proj_cols, consts, states_in, state_shapes, scratch, ybuf, prev_states, *,
                proj, layer, depth, ntok, row0, nb, L, C, NS):
    nch = L // C
    assert NS == 1 or nch == 1
    rows = NS * C
    rb0 = row0 // rows

    def rowmap(col):
        return lambda i, c: (rb0 + i * nch + c, col)

    def stmap(nd):
        return lambda i, c: (layer, i) + (0,) * nd

    in_specs = [pl.BlockSpec((rows, w), rowmap(off // w)) for w, off in proj_cols]
    args = [proj] * len(proj_cols)
    for arr, shp in consts:
        in_specs.append(pl.BlockSpec(shp, functools.partial(lambda i, c, nd: (0,) * nd, nd=len(shp))))
        args.append(arr)
    st_specs = [pl.BlockSpec((1, NS) + shp, stmap(len(shp))) for shp in state_shapes]
    for arr, spec in zip(states_in, st_specs):
        in_specs.append(spec)
        args.append(arr)
    al_arrays, al_specs, aliases = _alias_args(len(args), [ybuf] + list(prev_states))
    return pl.pallas_call(
        functools.partial(kernel_fn, C, NS, bool(states_in), len(al_arrays)),
        grid=(nb // NS, nch),
        in_specs=in_specs + al_specs,
        out_specs=[pl.BlockSpec((rows, MIX_W), rowmap(0))] + st_specs,
        out_shape=[jax.ShapeDtypeStruct((ntok, MIX_W), bf16)]
        + [jax.ShapeDtypeStruct((depth, nb) + shp, f32) for shp in state_shapes],
        scratch_shapes=scratch + [pltpu.VMEM((rows, MIX_W), f32)],
        input_output_aliases=aliases,
        compiler_params=_cparams("parallel", "arbitrary"),
        name=name + ("_state" if states_in else ""),
    )(*(args + al_arrays))


def _gdn_kernel(C, NS, has_state, n_alias, *refs):
    qkv_ref, z_ref, sm_ref, cw_ref, par_ref, gn_ref = refs[:6]
    n_in = 6
    if has_state:
        s0_ref, cs_ref = refs[6:8]
        n_in = 8
    y_ref, so_ref, cso_ref, ext_ref, yacc_ref = refs[n_in + n_alias:]
    c = pl.program_id(1)

    @pl.when(c == 0)
    def _():
        for s in range(NS):
            ext_ref[s, 0:8, :] = jnp.zeros((8, A_QKV), f32)
            if has_state:
                ext_ref[s, 5:8, :] = cs_ref[0, s]
        if has_state:
            so_ref[...] = s0_ref[...]
        else:
            so_ref[...] = jnp.zeros_like(so_ref)

    for s in range(NS):
        ext_ref[s, 8:8 + C, :] = qkv_ref[s * C:(s + 1) * C, :]

    def conv_silu(s, lo):
        acc = cw_ref[0:1, lo:lo + 128] * ext_ref[s, 5:5 + C, lo:lo + 128]
        for j in range(1, SHORT_CONV):
            acc = acc + cw_ref[j:j + 1, lo:lo + 128] * ext_ref[s, 5 + j:5 + j + C, lo:lo + 128]
        return _silu(acc)

    gates = []
    for s in range(NS):
        sm = sm_ref[s * C:(s + 1) * C, :]
        beta_all = _sigmoid(sm)
        g_all = -jnp.exp(par_ref[0:1, :]) * _softplus(sm + par_ref[1:2, :])
        gc_all = _cumsum_rows(g_all, C)
        gl_all = gc_all[C - 1:C, :]
        gates.append((beta_all, gc_all, jnp.exp(gc_all), jnp.exp(gl_all), jnp.exp(gl_all - gc_all)))

    ii = lax.broadcasted_iota(jnp.int32, (C, C), 0)
    jj = lax.broadcasted_iota(jnp.int32, (C, C), 1)
    eye = ii == jj
    tri = ii >= jj
    strict = ii > jj
    gain = gn_ref[...]

    def col(s, which, lane):
        return gates[s][which][:, lane:lane + 1]

    insts = [(s, h) for h in range(A_HEADS) for s in range(NS)]
    qs = [conv_silu(s, h * A_DK) for s, h in insts]
    qs = [q * lax.rsqrt(jnp.sum(q * q, axis=-1, keepdims=True) + EPS) * (A_DK ** -0.5) for q in qs]
    ks = [conv_silu(s, A_HEADS * A_DK + h * A_DK) for s, h in insts]
    ks = [k * lax.rsqrt(jnp.sum(k * k, axis=-1, keepdims=True) + EPS) for k in ks]
    vs = [conv_silu(s, 2 * A_HEADS * A_DK + h * A_DV) for s, h in insts]
    betas = [col(s, 0, LANE_AB + h) for s, h in insts]
    gcols = [col(s, 1, LANE_AA + h) for s, h in insts]
    egcs = [col(s, 2, LANE_AA + h) for s, h in insts]
    decays = [jnp.exp(jnp.where(tri, g - _col_to_row(g, eye), NEG_BIG)) for g in gcols]
    kbs = [k * b for k, b in zip(ks, betas)]
    lmats = [jnp.where(strict, _dot_nt(kb, k) * d, 0.0) for kb, k, d in zip(kbs, ks, decays)]
    rhss = [jnp.concatenate([v * b, kb * e], axis=-1) for v, b, kb, e in zip(vs, betas, kbs, egcs)]
    tinv = _inv_unit_lower(lmats, ii, jj, C)
    sols = [_mm3(_prep_l(_split(t)), _prep_r(_split(r))) for t, r in zip(tinv, rhss)]
    sts = [so_ref[0, s, h] for s, h in insts]
    vnews = [sol[:, :A_DV] - _dot(sol[:, A_DV:], st) for sol, st in zip(sols, sts)]
    qks = [jnp.where(tri, _dot_nt(q, k) * d, 0.0) for q, k, d in zip(qs, ks, decays)]
    outs = [_dot(q * e, st) + _dot(qk, vn) for q, e, st, qk, vn in zip(qs, egcs, sts, qks, vnews)]
    for (s, h), k, st, vn in zip(insts, ks, sts, vnews):
        so_ref[0, s, h] = st * col(s, 3, LANE_AA + h) + _dot_tn(k * col(s, 4, LANE_AA + h), vn)
    for (s, h), o in zip(insts, outs):
        o = o * lax.rsqrt(jnp.mean(o * o, axis=-1, keepdims=True) + EPS) * gain
        yacc_ref[s * C:(s + 1) * C, h * A_DV:(h + 1) * A_DV] = o * _silu(
            z_ref[s * C:(s + 1) * C, h * A_DV:(h + 1) * A_DV])
    y_ref[...] = yacc_ref[...].astype(y_ref.dtype)

    for s in range(NS):
        cso_ref[0, s] = ext_ref[s, C + 5:C + 8, :]
        ext_ref[s, 0:8, :] = ext_ref[s, C:C + 8, :]


def _gdn(proj, cw, par, gn, states_in, ybuf, prev_states, **kw):
    return _mixer_call(
        _gdn_kernel, "gdn",
        [(A_QKV, OFF_QKV), (MIX_W, OFF_AZ), (128, OFF_SMALL)],
        [(cw, (SHORT_CONV, A_QKV)), (par, (2, 128)), (gn, (1, A_DV))],
        states_in, [(A_HEADS, A_DK, A_DV), (SHORT_CONV - 1, A_QKV)],
        [pltpu.VMEM((kw["NS"], kw["C"] + 8, A_QKV), f32)], ybuf, prev_states, proj=proj, **kw)


def _ssd_kernel(C, NS, has_state, n_alias, *refs):
    z_ref, x_ref, bb_ref, bc_ref, sm_ref, cw_ref, cb_ref, par_ref, gn_ref = refs[:9]
    n_in = 9
    if has_state:
        s0_ref, cs_ref = refs[9:11]
        n_in = 11
    y_ref, so_ref, cso_ref, ext_ref, yacc_ref = refs[n_in + n_alias:]
    c = pl.program_id(1)
    ng = B_GROUPS * B_DSTATE

    @pl.when(c == 0)
    def _():
        for s in range(NS):
            ext_ref[s, 0:8, :] = jnp.zeros((8, B_CONV_DIM), f32)
            if has_state:
                ext_ref[s, 5:8, :] = cs_ref[0, s]
        if has_state:
            so_ref[...] = s0_ref[...]
        else:
            so_ref[...] = jnp.zeros_like(so_ref)

    ii = lax.broadcasted_iota(jnp.int32, (C, C), 0)
    jj = lax.broadcasted_iota(jnp.int32, (C, C), 1)
    eye = ii == jj
    tri = ii >= jj
    first = lax.broadcasted_iota(jnp.int32, (C, 128), 1) < B_HEADDIM
    first_lane = lax.broadcasted_iota(jnp.int32, (1, 128), 1) < B_HEADDIM
    first_row = lax.broadcasted_iota(jnp.int32, (2 * B_HEADDIM, 1), 0) < B_HEADDIM
    pairs_per_group = B_REP // 2

    for s in range(NS):
        r0 = s * C
        ext_ref[s, 8:8 + C, 0:B_DINNER] = x_ref[r0:r0 + C, :]
        ext_ref[s, 8:8 + C, B_DINNER:B_DINNER + ng] = bb_ref[r0:r0 + C, :]
        ext_ref[s, 8:8 + C, B_DINNER + ng:B_CONV_DIM] = bc_ref[r0:r0 + C, :]

    def conv_silu(s, lo):
        acc = cw_ref[0:1, lo:lo + 128] * ext_ref[s, 5:5 + C, lo:lo + 128]
        for j in range(1, SHORT_CONV):
            acc = acc + cw_ref[j:j + 1, lo:lo + 128] * ext_ref[s, 5 + j:5 + j + C, lo:lo + 128]
        return _silu(acc + cb_ref[0:1, lo:lo + 128])

    def pick(arr, l0, mask):
        return jnp.where(mask, arr[:, l0:l0 + 1], arr[:, l0 + 1:l0 + 2])

    gates = []
    for s in range(NS):
        sm = sm_ref[s * C:(s + 1) * C, :]
        dt_all = _softplus(sm + par_ref[1:2, :])
        gc_all = _cumsum_rows(dt_all * (-jnp.exp(par_ref[0:1, :])), C)
        gl_all = gc_all[C - 1:C, :]
        gates.append((dt_all, gc_all, jnp.exp(gc_all), jnp.exp(gl_all), jnp.exp(gl_all - gc_all)))
    d_all = par_ref[2:3, :]

    sg = [(s, g) for s in range(NS) for g in range(B_GROUPS)]
    bgs = {k: conv_silu(k[0], B_DINNER + k[1] * B_DSTATE) for k in sg}
    cgs = {k: conv_silu(k[0], B_DINNER + ng + k[1] * B_DSTATE) for k in sg}
    scores = {k: _dot_nt(cgs[k], bgs[k]) for k in sg}
    insts = [(s, p) for s in range(NS) for p in range(B_PAIRS)]
    lanes = [LANE_BDT + 2 * p for _, p in insts]
    grp = [(s, p // pairs_per_group) for s, p in insts]
    xps = [conv_silu(s, p * 128) for s, p in insts]
    vs = [xp * pick(gates[s][0], l0, first) for xp, (s, _), l0 in zip(xps, insts, lanes)]
    vbs = [v.astype(bf16) for v in vs]
    sds = []
    for (s, _), l0, k, vb in zip(insts, lanes, grp, vbs):
        pair = []
        for l in (l0, l0 + 1):
            gcol = gates[s][1][:, l:l + 1]
            decay = jnp.exp(jnp.where(tri, gcol - _col_to_row(gcol, eye), NEG_BIG))
            pair.append(jnp.dot((scores[k] * decay).astype(bf16), vb, preferred_element_type=f32))
        sds.append(pair)
    sts = [so_ref[0, s, p] for s, p in insts]
    outs = [jnp.where(first, sd[0], sd[1]) + _dot_nt(cgs[k], st) * pick(gates[s][2], l0, first)
            for sd, k, st, (s, _), l0 in zip(sds, grp, sts, insts, lanes)]
    for (s, p), l0, k, st, v in zip(insts, lanes, grp, sts, vs):
        so_ref[0, s, p] = st * pick(gates[s][3], l0, first_row) + _dot_tn(v * pick(gates[s][4], l0, first), bgs[k])
    yvs = [(o + pick(d_all, l0, first_lane) * xp) * _silu(z_ref[s * C:(s + 1) * C, p * 128:(p + 1) * 128])
           for o, l0, xp, (s, p) in zip(outs, lanes, xps, insts)]
    ssq = {k: jnp.zeros((C, 1), f32) for k in sg}
    for k, yv in zip(grp, yvs):
        ssq[k] = ssq[k] + jnp.sum(yv * yv, axis=-1, keepdims=True)
    scale = {k: lax.rsqrt(v * (1.0 / (B_DINNER // B_GROUPS)) + EPS) for k, v in ssq.items()}
    for (s, p), k, yv in zip(insts, grp, yvs):
        yacc_ref[s * C:(s + 1) * C, p * 128:(p + 1) * 128] = yv * scale[k] * gn_ref[0:1, p * 128:(p + 1) * 128]
    y_ref[...] = yacc_ref[...].astype(y_ref.dtype)

    for s in range(NS):
        cso_ref[0, s] = ext_ref[s, C + 5:C + 8, :]
        ext_ref[s, 0:8, :] = ext_ref[s, C:C + 8, :]


def _ssd(proj, cw, cb, par, gn, states_in, ybuf, prev_states, **kw):
    ng = B_GROUPS * B_DSTATE
    return _mixer_call(
        _ssd_kernel, "ssd",
        [(MIX_W, OFF_BZ), (MIX_W, OFF_BX), (ng, OFF_BB), (ng, OFF_BC), (128, OFF_SMALL)],
        [(cw, (SHORT_CONV, B_CONV_DIM)), (cb, (1, B_CONV_DIM)), (par, (3, 128)), (gn, (1, B_DINNER))],
        states_in, [(B_PAIRS, 2 * B_HEADDIM, B_DSTATE), (SHORT_CONV - 1, B_CONV_DIM)],
        [pltpu.VMEM((kw["NS"], kw["C"] + 8, B_CONV_DIM), f32)], ybuf, prev_states, proj=proj, **kw)


def _pack_ssm_state(s):
    lead = s.shape[:-3]
    return jnp.swapaxes(s, -1, -2).reshape(lead + (B_PAIRS, 2 * B_HEADDIM, B_DSTATE))


def _unpack_ssm_state(s):
    lead = s.shape[:-3]
    return jnp.swapaxes(s.reshape(lead + (B_HEADS, B_HEADDIM, B_DSTATE)), -1, -2)


_LOG_GAMMA = [float(np.log(np.float32(1.0) - np.float32(2.0) ** np.float32(-5.0 - h))) for h in range(C_HEADS)]


def _ret_kernel(pos0, C, NS, has_state, n_alias, *refs):
    q_ref, k_ref, v_ref, g_ref, inv_ref = refs[:5]
    n_in = 5
    if has_state:
        s0_ref = refs[5]
        n_in = 6
    y_ref, so_ref, yacc_ref = refs[n_in + n_alias:]
    c = pl.program_id(1)

    @pl.when(c == 0)
    def _():
        if has_state:
            so_ref[...] = s0_ref[...]
        else:
            so_ref[...] = jnp.zeros_like(so_ref)

    half = C_DK // 2
    pos = (pos0 + c * C + lax.broadcasted_iota(jnp.int32, (C, half), 0)).astype(f32)
    ang = pos * inv_ref[...]
    cos = jnp.cos(ang)
    sin = jnp.sin(ang)

    ii = lax.broadcasted_iota(jnp.int32, (C, C), 0)
    jj = lax.broadcasted_iota(jnp.int32, (C, C), 1)
    tri = ii >= jj
    dij = (ii - jj).astype(f32)
    ipos = (lax.broadcasted_iota(jnp.int32, (C, 1), 0) + 1).astype(f32)

    for s in range(NS):
        r0 = s * C

        def rot(ref, h, r0=r0):
            t1 = ref[r0:r0 + C, h * C_DK:h * C_DK + half]
            t2 = ref[r0:r0 + C, h * C_DK + half:(h + 1) * C_DK]
            return jnp.concatenate([t1 * cos - t2 * sin, t1 * sin + t2 * cos], axis=-1)

        for h in range(C_HEADS):
            lg = _LOG_GAMMA[h]
            q = rot(q_ref, h)
            k = rot(k_ref, h) * (C_DK ** -0.5)
            v = v_ref[r0:r0 + C, h * C_DV:(h + 1) * C_DV]
            decay = jnp.exp(jnp.where(tri, dij * lg, NEG_BIG))
            egc = jnp.exp(ipos * lg)
            edl = jnp.exp((float(C) - ipos) * lg)
            egl = float(np.exp(np.float32(C * lg)))
            st = so_ref[0, s, h]
            o = _dot(_dot_nt(q, k) * decay, v) + _dot(q, st) * egc
            so_ref[0, s, h] = st * egl + _dot_tn(k, v * edl)
            o = o * lax.rsqrt(jnp.mean(o * o, axis=-1, keepdims=True) + EPS)
            yacc_ref[r0:r0 + C, h * C_DV:(h + 1) * C_DV] = o * _silu(g_ref[r0:r0 + C, h * C_DV:(h + 1) * C_DV])
    y_ref[...] = yacc_ref[...].astype(y_ref.dtype)


def _ret(proj, inv, states_in, ybuf, prev_states, *, pos0, **kw):
    return _mixer_call(
        functools.partial(_ret_kernel, pos0), "ret",
        [(MIX_W, OFF_CQ), (MIX_W, OFF_CK), (MIX_W, OFF_CV), (MIX_W, OFF_CG)],
        [(inv, (1, C_DK // 2))],
        states_in, [(C_HEADS, C_DK, C_DV)], [], ybuf, prev_states, proj=proj, **kw)


FFN_TN = 256


def _ffn_gu_prompt_kernel(tm, tiles_per_seq, n_alias, *refs):
    u_ref, wg_ref, wu_ref, cw_ref, cb_ref = refs[:5]
    a_ref, cso_ref, wgb_ref, wub_ref, ext_ref = refs[5 + n_alias:]
    i = pl.program_id(1)

    @pl.when(i == 0)
    def _():
        wgb_ref[...] = wg_ref[...].astype(bf16)
        wub_ref[...] = wu_ref[...].astype(bf16)

    @pl.when(i % tiles_per_seq == 0)
    def _():
        ext_ref[0:8, :] = jnp.zeros((8, FFN_TN), f32)

    u = u_ref[...]
    ext_ref[8:8 + tm, :] = jnp.dot(u, wgb_ref[...], preferred_element_type=f32)
    acc = cw_ref[0:1, :] * ext_ref[6:6 + tm, :]
    for j in range(1, FFN_CONV):
        acc = acc + cw_ref[j:j + 1, :] * ext_ref[6 + j:6 + j + tm, :]
    acc = _silu(acc + cb_ref[...])
    a_ref[...] = (acc * jnp.dot(u, wub_ref[...], preferred_element_type=f32)).astype(a_ref.dtype)
    cso_ref[0, 0] = ext_ref[tm + 6:tm + 8, :]
    ext_ref[0:8, :] = ext_ref[tm:tm + 8, :]


def _ffn_gu_sample_kernel(nb, L, n_alias, *refs):
    u_ref, wg_ref, wu_ref, cw_ref, cb_ref, cs_ref = refs[:6]
    a_ref, cso_ref, ext_ref = refs[6 + n_alias:]
    u = u_ref[...]
    gate = jnp.dot(u, wg_ref[...].astype(bf16), preferred_element_type=f32)
    ext_ref[:, 0:8, :] = jnp.zeros((nb, 8, FFN_TN), f32)
    ext_ref[:, 6:8, :] = cs_ref[0]
    ext_ref[:, 8:8 + L, :] = gate.reshape(nb, L, FFN_TN)
    acc = cw_ref[0:1, :] * ext_ref[:, 6:6 + L, :]
    for j in range(1, FFN_CONV):
        acc = acc + cw_ref[j:j + 1, :] * ext_ref[:, 6 + j:6 + j + L, :]
    acc = _silu(acc + cb_ref[...])
    up = jnp.dot(u, wu_ref[...].astype(bf16), preferred_element_type=f32)
    a_ref[...] = (acc.reshape(nb * L, FFN_TN) * up).astype(a_ref.dtype)
    cso_ref[0] = ext_ref[:, L + 6:L + 8, :]


def _ffn_gu(u2, wg, wu, cw, cb, cs, abuf, prev_state, *, layer, depth, ntok, row0, nb, L):
    d = u2.shape[1]
    nf = wg.shape[-1]
    tn = FFN_TN
    w_spec = pl.BlockSpec((None, d, tn), lambda j, i: (layer, 0, j))
    cw_spec = pl.BlockSpec((None, FFN_CONV, tn), lambda j, i: (layer, 0, j))
    cb_spec = pl.BlockSpec((None, 1, tn), lambda j, i: (layer, 0, j))
    args = [u2, wg, wu, cw, cb]
    if cs is None:
        tm = _tile(L, 1024, 16)
        tps = L // tm
        rb0 = row0 // tm
        grid = (nf // tn, nb * tps)
        in_specs = [pl.BlockSpec((tm, d), lambda j, i: (rb0 + i, 0)), w_spec, w_spec, cw_spec, cb_spec]
        cs_spec = pl.BlockSpec((1, 1, FFN_CONV - 1, tn), lambda j, i: (layer, i // tps, 0, j))
        body = functools.partial(_ffn_gu_prompt_kernel, tm, tps)
        scratch = [pltpu.VMEM((d, tn), bf16), pltpu.VMEM((d, tn), bf16), pltpu.VMEM((tm + 8, tn), f32)]
        name = "ffn_gu"
    else:
        tm = nb * L
        assert row0 % tm == 0 and L % 8 == 0
        rb0 = row0 // tm
        grid = (nf // tn, 1)
        cs_spec = pl.BlockSpec((1, nb, FFN_CONV - 1, tn), lambda j, i: (layer, 0, 0, j))
        in_specs = [pl.BlockSpec((tm, d), lambda j, i: (rb0, 0)), w_spec, w_spec, cw_spec, cb_spec, cs_spec]
        args.append(cs)
        body = functools.partial(_ffn_gu_sample_kernel, nb, L)
        scratch = [pltpu.VMEM((nb, L + 8, tn), f32)]
        name = "ffn_gu_state"
    al_arrays, al_specs, aliases = _alias_args(len(args), [abuf, prev_state])
    return pl.pallas_call(
        functools.partial(body, len(al_arrays)),
        grid=grid,
        in_specs=in_specs + al_specs,
        out_specs=[pl.BlockSpec((tm, tn), lambda j, i: (rb0 + i, j)), cs_spec],
        out_shape=[jax.ShapeDtypeStruct((ntok, nf), bf16),
                   jax.ShapeDtypeStruct((depth, nb, FFN_CONV - 1, nf), f32)],
        scratch_shapes=scratch,
        input_output_aliases=aliases,
        compiler_params=_cparams("parallel", "arbitrary"),
        name=name,
    )(*(args + al_arrays))


def _lane_row(vals, lane0):
    return jnp.zeros((128,), f32).at[lane0:lane0 + vals.shape[0]].set(vals.astype(f32))


def kernel(x_prompt, x_sample, state_gdn, state_gdn_conv, state_ssm, state_ssm_conv, state_ret, state_ffn_conv, norm_mix, w_in, gdn_conv_w, gdn_a_log, gdn_dt_bias, gdn_norm, ssm_conv_w, ssm_conv_b, ssm_a_log, ssm_dt_bias, ssm_d, ssm_norm, w_branch_a, w_branch_b, w_branch_c, w_out, norm_ffn, w_ffn_gate, w_ffn_up, ffn_conv_w, ffn_conv_b, w_ffn_down, norm_final):
    nbp, lp, d = x_prompt.shape
    nbs, ls, _ = x_sample.shape
    depth = w_in.shape[0]
    tp = nbp * lp
    ts = nbs * ls
    ntok = tp + ts
    cp = CHUNK if lp % CHUNK == 0 else lp
    cs_len = CHUNK if ls % CHUNK == 0 else ls
    ns = SAMPLE_SEQS_PER_STEP

    x = jnp.concatenate([x_prompt.reshape(tp, d), x_sample.reshape(ts, d)], axis=0)
    half = C_DK // 2
    inv = (ROPE_BASE ** (-jnp.arange(half, dtype=f32) / half)).reshape(1, half)
    ssm_in = _pack_ssm_state(state_ssm)
    w_in_t = jnp.swapaxes(w_in, 1, 2)
    wa_bf, wb_bf, wc_bf = w_branch_a.astype(bf16), w_branch_b.astype(bf16), w_branch_c.astype(bf16)
    wd_bf = w_ffn_down.astype(bf16)
    fcb = ffn_conv_b.reshape(depth, 1, D_FF)

    gdn_p = gdn_conv_p = ssm_p = ssm_conv_p = ret_p = ffn_conv_p = None
    gdn_s = gdn_conv_s = ssm_s = ssm_conv_s = ret_s = ffn_conv_s = None
    for l in range(depth):
        kp = dict(layer=l, depth=depth, ntok=ntok, row0=0, nb=nbp, L=lp, C=cp, NS=1)
        ks = dict(layer=l, depth=depth, ntok=ntok, row0=tp, nb=nbs, L=ls, C=cs_len, NS=ns)
        u = _rmsnorm(x, norm_mix[l], bf16)
        proj = _in_proj(u, w_in_t, l)

        gdn_par = jnp.stack([_lane_row(gdn_a_log[l], LANE_AA), _lane_row(gdn_dt_bias[l], LANE_AA)])
        gdn_gn = gdn_norm[l].reshape(1, A_DV)
        ya, gdn_p, gdn_conv_p = _gdn(proj, gdn_conv_w[l], gdn_par, gdn_gn, [], None, [gdn_p, gdn_conv_p], **kp)
        ya, gdn_s, gdn_conv_s = _gdn(proj, gdn_conv_w[l], gdn_par, gdn_gn, [state_gdn, state_gdn_conv], ya,
                                     [gdn_s, gdn_conv_s], **ks)

        ssd_par = jnp.stack([_lane_row(ssm_a_log[l], LANE_BDT), _lane_row(ssm_dt_bias[l], LANE_BDT),
                             _lane_row(ssm_d[l], LANE_BDT)])
        ssd_cb = ssm_conv_b[l].reshape(1, B_CONV_DIM)
        ssd_gn = ssm_norm[l].reshape(1, B_DINNER)
        yb, ssm_p, ssm_conv_p = _ssd(proj, ssm_conv_w[l], ssd_cb, ssd_par, ssd_gn, [], None, [ssm_p, ssm_conv_p], **kp)
        yb, ssm_s, ssm_conv_s = _ssd(proj, ssm_conv_w[l], ssd_cb, ssd_par, ssd_gn, [ssm_in, state_ssm_conv], yb,
                                     [ssm_s, ssm_conv_s], **ks)

        yc, ret_p = _ret(proj, inv, [], None, [ret_p], pos0=0, **kp)
        yc, ret_s = _ret(proj, inv, [state_ret], yc, [ret_s], pos0=PAST_LEN, **ks)

        h = _merge(ya, yb, yc, wa_bf, wb_bf, wc_bf, proj, l)
        x = _matmul(h, w_out, res=x, layer=l, name="out_proj")

        u2 = _rmsnorm(x, norm_ffn[l], bf16)
        kf = dict(layer=l, depth=depth, ntok=ntok)
        act, ffn_conv_p = _ffn_gu(u2, w_ffn_gate, w_ffn_up, ffn_conv_w, fcb, None, None, ffn_conv_p,
                                  row0=0, nb=nbp, L=lp, **kf)
        act, ffn_conv_s = _ffn_gu(u2, w_ffn_gate, w_ffn_up, ffn_conv_w, fcb, state_ffn_conv, act, ffn_conv_s,
                                  row0=tp, nb=nbs, L=ls, **kf)
        x = _matmul(act, wd_bf, res=x, layer=l, tk=D_FF // 2, name="ffn_down")

    y_prompt = _rmsnorm(x, norm_final, f32, 0, tp).reshape(nbp, lp, d)
    y_sample = _rmsnorm(x, norm_final, f32, tp, ts).reshape(nbs, ls, d)
    return (y_prompt, y_sample,
            gdn_p, gdn_conv_p, _unpack_ssm_state(ssm_p), ssm_conv_p, ret_p, ffn_conv_p,
            gdn_s, gdn_conv_s, _unpack_ssm_state(ssm_s), ssm_conv_s, ret_s, ffn_conv_s)
```

```python
import functools

import numpy as np
import jax
import jax.numpy as jnp
from jax import lax
from jax.experimental import pallas as pl
from jax.experimental.pallas import tpu as pltpu

f32 = jnp.float32
bf16 = jnp.bfloat16

D_MODEL = 4096
A_HEADS, A_DK, A_DV = 16, 128, 128
A_QKV = A_HEADS * (2 * A_DK + A_DV)
B_HEADS, B_HEADDIM, B_GROUPS, B_DSTATE = 32, 64, 4, 128
B_DINNER = B_HEADS * B_HEADDIM
B_REP = B_HEADS // B_GROUPS
B_PAIRS = B_HEADS // 2
B_CONV_DIM = B_DINNER + 2 * B_GROUPS * B_DSTATE
C_HEADS, C_DK, C_DV = 8, 256, 256
SHORT_CONV = 4
FFN_CONV = 3
CHUNK = 64
D_FF = 11008
ROPE_BASE = 10000.0
EPS = 1e-6
PAST_LEN = 16384
MIX_W = 2048

OFF_QKV = 0
OFF_AZ = 6144
OFF_BZ = 8192
OFF_BX = 10240
OFF_CQ = 12288
OFF_CK = 14336
OFF_CV = 16384
OFF_CG = 18432
OFF_GATES = 20480
OFF_BB = 32768
OFF_BC = 33280
OFF_SMALL = 33792
NP = 34304
LANE_AB, LANE_AA, LANE_BDT = 0, 16, 32

VMEM_LIMIT = 56 * 1024 * 1024
NEG_BIG = -1e30
SAMPLE_SEQS_PER_STEP = 2


def _cparams(*sem):
    return pltpu.CompilerParams(dimension_semantics=sem, vmem_limit_bytes=VMEM_LIMIT)


def _tile(n, pref, mult):
    t = min(pref, n)
    t -= t % mult
    while t >= mult:
        if n % t == 0:
            return t
        t -= mult
    return n


def _sigmoid(x):
    return 1.0 / (1.0 + jnp.exp(-x))


def _silu(x):
    return x * _sigmoid(x)


def _softplus(x):
    return jnp.maximum(x, 0.0) + jnp.log(1.0 + jnp.exp(-jnp.abs(x)))


def _dot(a, b):
    return jnp.dot(a.astype(bf16), b.astype(bf16), preferred_element_type=f32)


def _dot_nt(a, b):
    return lax.dot_general(a.astype(bf16), b.astype(bf16), (((1,), (1,)), ((), ())), preferred_element_type=f32)


def _dot_tn(a, b):
    return lax.dot_general(a.astype(bf16), b.astype(bf16), (((0,), (0,)), ((), ())), preferred_element_type=f32)


def _split(a):
    hi = lax.bitcast_convert_type(lax.bitcast_convert_type(a, jnp.int32) & jnp.int32(-65536), f32)
    return hi, a - hi


def _prep_l(sa):
    hi, lo = sa
    if hi.shape[1] == 64:
        return (jnp.concatenate([hi, lo, hi], axis=1).astype(bf16),)
    return hi.astype(bf16), lo.astype(bf16)


def _prep_r(sb):
    hi, lo = sb
    if hi.shape[0] == 64:
        return (jnp.concatenate([hi, hi, lo], axis=0).astype(bf16),)
    return hi.astype(bf16), lo.astype(bf16)


def _mm3(l, r):
    d = functools.partial(jnp.dot, preferred_element_type=f32)
    if len(l) == 1:
        return d(l[0], r[0])
    return d(l[0], r[0]) + (d(l[1], r[0]) + d(l[0], r[1]))


def _cumsum_rows(x, n):
    row = lax.broadcasted_iota(jnp.int32, x.shape, 0)
    s = 1
    while s < n:
        x = x + jnp.where(row >= s, pltpu.roll(x, s, axis=0), 0.0)
        s *= 2
    return x


def _col_to_row(col, eye):
    return jnp.sum(jnp.where(eye, col, 0.0), axis=0, keepdims=True)


def _inv_unit_lower(lmats, ii, jj, n):
    eye = (ii == jj).astype(f32)
    blk = (ii >> 3) == (jj >> 3)
    ps = [jnp.where(blk, -l, 0.0) for l in lmats]
    ts = [eye + p for p in ps]
    sp = [_split(p) for p in ps]
    p2 = [_mm3(_prep_l(s), _prep_r(s)) for s in sp]
    sp2 = [_split(q) for q in p2]
    p2r = [_prep_r(s) for s in sp2]
    ts = [t + _mm3(_prep_l(_split(t)), r) for t, r in zip(ts, p2r)]
    p4 = [_mm3(_prep_l(s), r) for s, r in zip(sp2, p2r)]
    ts = [t + _mm3(_prep_l(_split(t)), _prep_r(_split(q))) for t, q in zip(ts, p4)]
    s = 8
    while s < n:
        sh = s.bit_length() - 1
        mask = ((ii >> (sh + 1)) == (jj >> (sh + 1))) & ((ii >> sh) != (jj >> sh))
        st = [_split(t) for t in ts]
        mids = [_mm3(_prep_l(_split(jnp.where(mask, l, 0.0))), _prep_r(s_)) for l, s_ in zip(lmats, st)]
        ts = [t - _mm3(_prep_l(s_), _prep_r(_split(m))) for t, s_, m in zip(ts, st, mids)]
        s *= 2
    return ts


def _rmsnorm_kernel(x_ref, g_ref, o_ref):
    x = x_ref[...]
    y = x * lax.rsqrt(jnp.mean(x * x, axis=-1, keepdims=True) + EPS) * g_ref[...]
    o_ref[...] = y.astype(o_ref.dtype)


def _rmsnorm(x, gain, out_dtype):
    t, d = x.shape
    tm = _tile(t, 256, 16)
    return pl.pallas_call(
        _rmsnorm_kernel,
        grid=(t // tm,),
        in_specs=[pl.BlockSpec((tm, d), lambda i: (i, 0)), pl.BlockSpec((1, d), lambda i: (0, 0))],
        out_specs=pl.BlockSpec((tm, d), lambda i: (i, 0)),
        out_shape=jax.ShapeDtypeStruct((t, d), out_dtype),
        compiler_params=_cparams("parallel"),
        name="rmsnorm",
    )(x, gain.reshape(1, d))


def _mm_kernel(nk, has_res, *refs):
    a_ref, b_ref = refs[:2]
    r_ref = refs[2] if has_res else None
    o_ref = refs[2 + int(has_res)]
    if nk == 1:
        acc = jnp.dot(a_ref[...], b_ref[...].astype(bf16), preferred_element_type=f32)
        if has_res:
            acc = r_ref[...] + acc
        o_ref[...] = acc.astype(o_ref.dtype)
        return
    acc_ref = refs[-1]
    k = pl.program_id(2)

    @pl.when(k == 0)
    def _():
        acc_ref[...] = jnp.zeros_like(acc_ref)

    acc_ref[...] += jnp.dot(a_ref[...], b_ref[...].astype(bf16), preferred_element_type=f32)

    @pl.when(k == nk - 1)
    def _():
        acc = acc_ref[...]
        if has_res:
            acc = r_ref[...] + acc
        o_ref[...] = acc.astype(o_ref.dtype)


def _matmul(a, b, res=None, *, layer=None, tm=1024, tn=512, tk=None, out_dtype=f32, name="matmul"):
    m, kdim = a.shape
    n = b.shape[-1]
    tm = _tile(m, tm, 16)
    tn = _tile(n, tn, 128)
    tk = kdim if tk is None else tk
    nk = kdim // tk
    if layer is None:
        b_spec = pl.BlockSpec((tk, tn), lambda i, j, k: (k, j))
    else:
        b_spec = pl.BlockSpec((None, tk, tn), lambda i, j, k: (layer, k, j))
    in_specs = [pl.BlockSpec((tm, tk), lambda i, j, k: (i, k)), b_spec]
    args = [a, b]
    if res is not None:
        in_specs.append(pl.BlockSpec((tm, tn), lambda i, j, k: (i, j)))
        args.append(res)
    return pl.pallas_call(
        functools.partial(_mm_kernel, nk, res is not None),
        grid=(m // tm, n // tn, nk),
        in_specs=in_specs,
        out_specs=pl.BlockSpec((tm, tn), lambda i, j, k: (i, j)),
        out_shape=jax.ShapeDtypeStruct((m, n), out_dtype),
        scratch_shapes=[pltpu.VMEM((tm, tn), f32)] if nk > 1 else [],
        compiler_params=_cparams("parallel", "parallel", "arbitrary"),
        name=name,
    )(*args)


IN_PROJ_TN = 512


def _in_proj_kernel(n_main, offs_ref, u_ref, wt_ref, ws_ref, o_ref):
    j = pl.program_id(1)

    @pl.when(j < n_main)
    def _():
        o_ref[...] = _dot_nt(u_ref[...], wt_ref[0])

    @pl.when(j == n_main)
    def _():
        o_ref[...] = _dot_nt(u_ref[...], ws_ref[...])


def _in_proj(u, w_in_t, layer):
    o_ab = A_QKV + A_HEADS * A_DV
    o_bz = o_ab + 2 * A_HEADS
    o_bbc = o_bz + 2 * B_DINNER
    o_dt = o_bbc + 2 * B_GROUPS * B_DSTATE
    o_cq = o_dt + B_HEADS
    total = w_in_t.shape[1]
    tn = IN_PROJ_TN
    runs = [(OFF_QKV, 0, o_ab), (OFF_BZ, o_bz, o_bbc - o_bz), (OFF_CQ, o_cq, total - o_cq), (OFF_BB, o_bbc, o_dt - o_bbc)]
    offs = []
    for col0, src, width in runs:
        assert col0 == len(offs) * tn and width % tn == 0 and src % 32 == 0
        offs += [src + t * tn for t in range(width // tn)]
    assert len(offs) * tn == OFF_SMALL and NP == OFF_SMALL + tn
    n_main = len(offs)
    offs.append(0)
    m, kdim = u.shape
    tm = _tile(m, 1024, 16)
    small = jnp.concatenate([w_in_t[layer, o_ab:o_bz, :], w_in_t[layer, o_dt:o_cq, :],
                             jnp.zeros((tn - (o_bz - o_ab) - (o_cq - o_dt), kdim), w_in_t.dtype)], axis=0)
    grid_spec = pltpu.PrefetchScalarGridSpec(
        num_scalar_prefetch=1, grid=(m // tm, n_main + 1),
        in_specs=[pl.BlockSpec((pl.Element(tm), pl.Element(kdim)), lambda i, j, o: (i * tm, 0)),
                  pl.BlockSpec((pl.Element(1), pl.Element(tn), pl.Element(kdim)),
                               lambda i, j, o: (layer, pl.multiple_of(o[j], 32), 0)),
                  pl.BlockSpec((pl.Element(tn), pl.Element(kdim)), lambda i, j, o: (0, 0))],
        out_specs=pl.BlockSpec((tm, tn), lambda i, j, o: (i, j)))
    return pl.pallas_call(
        functools.partial(_in_proj_kernel, n_main), grid_spec=grid_spec,
        out_shape=jax.ShapeDtypeStruct((m, NP), f32),
        compiler_params=_cparams("parallel", "arbitrary"),
        name="in_proj",
    )(jnp.asarray(offs, jnp.int32), u, w_in_t, small)


def _merge_kernel(ya_ref, yb_ref, yc_ref, wa_ref, wb_ref, wc_ref, ga_ref, gb_ref, gc_ref, o_ref):
    d = functools.partial(jnp.dot, preferred_element_type=f32)
    h = _sigmoid(ga_ref[...]) * d(ya_ref[...], wa_ref[...])
    h = h + _sigmoid(gb_ref[...]) * d(yb_ref[...], wb_ref[...])
    h = h + _sigmoid(gc_ref[...]) * d(yc_ref[...], wc_ref[...])
    o_ref[...] = h.astype(o_ref.dtype)


def _merge(ya, yb, yc, wa, wb, wc, proj, layer):
    t, kd = ya.shape
    n = wa.shape[-1]
    tm = _tile(t, 512, 16)
    tn = 512
    g0 = OFF_GATES // tn
    gstep = n // tn
    yspec = pl.BlockSpec((tm, kd), lambda i, j: (i, 0))
    wspec = pl.BlockSpec((None, kd, tn), lambda i, j: (layer, 0, j))
    gspecs = [pl.BlockSpec((tm, tn), functools.partial(lambda i, j, o: (i, o + j), o=g0 + r * gstep)) for r in range(3)]
    return pl.pallas_call(
        _merge_kernel,
        grid=(t // tm, n // tn),
        in_specs=[yspec, yspec, yspec, wspec, wspec, wspec] + gspecs,
        out_specs=pl.BlockSpec((tm, tn), lambda i, j: (i, j)),
        out_shape=jax.ShapeDtypeStruct((t, n), bf16),
        compiler_params=_cparams("parallel", "parallel"),
        name="merge",
    )(ya, yb, yc, wa, wb, wc, proj, proj, proj)


def _mixer_call(kernel_fn, name, proj_cols, consts, states_in, state_shapes, scratch, prev_states, *,
                proj, layer, nb, L, C, NS):
    nch = L // C
    assert NS == 1 or nch == 1

    def rowmap(col):
        return lambda i, c: (i * nch + c, col)

    def stspec(nlayers, shp, l0):
        return pl.BlockSpec((nlayers, NS) + shp, functools.partial(lambda i, c, nd: (l0, i) + (0,) * nd, nd=len(shp)))

    rows = NS * C
    in_specs = [pl.BlockSpec((rows, w), rowmap(off // w)) for w, off in proj_cols]
    args = [proj] * len(proj_cols)
    for arr, shp in consts:
        in_specs.append(pl.BlockSpec(shp, functools.partial(lambda i, c, nd: (0,) * nd, nd=len(shp))))
        args.append(arr)
    for arr, shp in zip(states_in, state_shapes):
        in_specs.append(stspec(1, shp, layer))
        args.append(arr)
    if layer:
        for arr, shp in zip(prev_states, state_shapes):
            in_specs.append(stspec(layer, shp, 0))
            args.append(arr)
    return pl.pallas_call(
        functools.partial(kernel_fn, C, NS, bool(states_in), layer),
        grid=(nb // NS, nch),
        in_specs=in_specs,
        out_specs=[pl.BlockSpec((rows, MIX_W), rowmap(0))] + [stspec(layer + 1, shp, 0) for shp in state_shapes],
        out_shape=[jax.ShapeDtypeStruct((nb * L, MIX_W), bf16)]
        + [jax.ShapeDtypeStruct((layer + 1, nb) + shp, f32) for shp in state_shapes],
        scratch_shapes=scratch + [pltpu.VMEM((rows, MIX_W), f32)],
        compiler_params=_cparams("parallel", "arbitrary"),
        name=name + ("_state" if states_in else ""),
    )(*args)


def _gdn_kernel(C, NS, has_state, layer, *refs):
    qkv_ref, z_ref, sm_ref, cw_ref, par_ref, gn_ref = refs[:6]
    n_in = 6
    if has_state:
        s0_ref, cs_ref = refs[6:8]
        n_in = 8
    if layer:
        prev_so_ref, prev_cso_ref = refs[n_in:n_in + 2]
        n_in += 2
    y_ref, so_ref, cso_ref, ext_ref, yacc_ref = refs[n_in:]
    c = pl.program_id(1)

    @pl.when(c == 0)
    def _():
        for s in range(NS):
            ext_ref[s, 0:8, :] = jnp.zeros((8, A_QKV), f32)
            if has_state:
                ext_ref[s, 5:8, :] = cs_ref[0, s]
        so_ref[layer] = s0_ref[0] if has_state else jnp.zeros(so_ref.shape[1:], f32)
        if layer:
            so_ref[0:layer] = prev_so_ref[...]
            cso_ref[0:layer] = prev_cso_ref[...]

    for s in range(NS):
        ext_ref[s, 8:8 + C, :] = qkv_ref[s * C:(s + 1) * C, :]

    def conv_silu(s, lo):
        acc = cw_ref[0:1, lo:lo + 128] * ext_ref[s, 5:5 + C, lo:lo + 128]
        for j in range(1, SHORT_CONV):
            acc = acc + cw_ref[j:j + 1, lo:lo + 128] * ext_ref[s, 5 + j:5 + j + C, lo:lo + 128]
        return _silu(acc)

    gates = []
    for s in range(NS):
        sm = sm_ref[s * C:(s + 1) * C, :]
        beta_all = _sigmoid(sm)
        g_all = -jnp.exp(par_ref[0:1, :]) * _softplus(sm + par_ref[1:2, :])
        gc_all = _cumsum_rows(g_all, C)
        gl_all = gc_all[C - 1:C, :]
        gates.append((beta_all, gc_all, jnp.exp(gc_all), jnp.exp(gl_all), jnp.exp(gl_all - gc_all)))

    ii = lax.broadcasted_iota(jnp.int32, (C, C), 0)
    jj = lax.broadcasted_iota(jnp.int32, (C, C), 1)
    eye = ii == jj
    tri = ii >= jj
    strict = ii > jj
    gain = gn_ref[...]

    def col(s, which, lane):
        return gates[s][which][:, lane:lane + 1]

    insts = [(s, h) for h in range(A_HEADS) for s in range(NS)]
    qs = [conv_silu(s, h * A_DK) for s, h in insts]
    qs = [q * lax.rsqrt(jnp.sum(q * q, axis=-1, keepdims=True) + EPS) * (A_DK ** -0.5) for q in qs]
    ks = [conv_silu(s, A_HEADS * A_DK + h * A_DK) for s, h in insts]
    ks = [k * lax.rsqrt(jnp.sum(k * k, axis=-1, keepdims=True) + EPS) for k in ks]
    vs = [conv_silu(s, 2 * A_HEADS * A_DK + h * A_DV) for s, h in insts]
    betas = [col(s, 0, LANE_AB + h) for s, h in insts]
    gcols = [col(s, 1, LANE_AA + h) for s, h in insts]
    egcs = [col(s, 2, LANE_AA + h) for s, h in insts]
    decays = [jnp.exp(jnp.where(tri, g - _col_to_row(g, eye), NEG_BIG)) for g in gcols]
    kbs = [k * b for k, b in zip(ks, betas)]
    lmats = [jnp.where(strict, _dot_nt(kb, k) * d, 0.0) for kb, k, d in zip(kbs, ks, decays)]
    rhss = [jnp.concatenate([v * b, kb * e], axis=-1) for v, b, kb, e in zip(vs, betas, kbs, egcs)]
    tinv = _inv_unit_lower(lmats, ii, jj, C)
    sols = [_mm3(_prep_l(_split(t)), _prep_r(_split(r))) for t, r in zip(tinv, rhss)]
    sts = [so_ref[layer, s, h] for s, h in insts]
    vnews = [sol[:, :A_DV] - _dot(sol[:, A_DV:], st) for sol, st in zip(sols, sts)]
    qks = [jnp.where(tri, _dot_nt(q, k) * d, 0.0) for q, k, d in zip(qs, ks, decays)]
    outs = [_dot(q * e, st) + _dot(qk, vn) for q, e, st, qk, vn in zip(qs, egcs, sts, qks, vnews)]
    for (s, h), k, st, vn in zip(insts, ks, sts, vnews):
        so_ref[layer, s, h] = st * col(s, 3, LANE_AA + h) + _dot_tn(k * col(s, 4, LANE_AA + h), vn)
    for (s, h), o in zip(insts, outs):
        o = o * lax.rsqrt(jnp.mean(o * o, axis=-1, keepdims=True) + EPS) * gain
        yacc_ref[s * C:(s + 1) * C, h * A_DV:(h + 1) * A_DV] = o * _silu(
            z_ref[s * C:(s + 1) * C, h * A_DV:(h + 1) * A_DV])
    y_ref[...] = yacc_ref[...].astype(y_ref.dtype)

    for s in range(NS):
        cso_ref[layer, s] = ext_ref[s, C + 5:C + 8, :]
        ext_ref[s, 0:8, :] = ext_ref[s, C:C + 8, :]


def _gdn(proj, cw, par, gn, states_in, prev_states, **kw):
    return _mixer_call(
        _gdn_kernel, "gdn",
        [(A_QKV, OFF_QKV), (MIX_W, OFF_AZ), (128, OFF_SMALL)],
        [(cw, (SHORT_CONV, A_QKV)), (par, (2, 128)), (gn, (1, A_DV))],
        states_in, [(A_HEADS, A_DK, A_DV), (SHORT_CONV - 1, A_QKV)],
        [pltpu.VMEM((kw["NS"], kw["C"] + 8, A_QKV), f32)], prev_states, proj=proj, **kw)


def _ssd_kernel(C, NS, has_state, layer, *refs):
    z_ref, x_ref, bb_ref, bc_ref, sm_ref, cw_ref, cb_ref, par_ref, gn_ref = refs[:9]
    n_in = 9
    if has_state:
        s0_ref, cs_ref = refs[9:11]
        n_in = 11
    if layer:
        prev_so_ref, prev_cso_ref = refs[n_in:n_in + 2]
        n_in += 2
    y_ref, so_ref, cso_ref, ext_ref, yacc_ref = refs[n_in:]
    c = pl.program_id(1)
    ng = B_GROUPS * B_DSTATE

    @pl.when(c == 0)
    def _():
        for s in range(NS):
            ext_ref[s, 0:8, :] = jnp.zeros((8, B_CONV_DIM), f32)
            if has_state:
                ext_ref[s, 5:8, :] = cs_ref[0, s]
        so_ref[layer] = s0_ref[0] if has_state else jnp.zeros(so_ref.shape[1:], f32)
        if layer:
            so_ref[0:layer] = prev_so_ref[...]
            cso_ref[0:layer] = prev_cso_ref[...]

    ii = lax.broadcasted_iota(jnp.int32, (C, C), 0)
    jj = lax.broadcasted_iota(jnp.int32, (C, C), 1)
    eye = ii == jj
    tri = ii >= jj
    first = lax.broadcasted_iota(jnp.int32, (C, 128), 1) < B_HEADDIM
    first_lane = lax.broadcasted_iota(jnp.int32, (1, 128), 1) < B_HEADDIM
    first_row = lax.broadcasted_iota(jnp.int32, (2 * B_HEADDIM, 1), 0) < B_HEADDIM
    pairs_per_group = B_REP // 2

    for s in range(NS):
        r0 = s * C
        ext_ref[s, 8:8 + C, 0:B_DINNER] = x_ref[r0:r0 + C, :]
        ext_ref[s, 8:8 + C, B_DINNER:B_DINNER + ng] = bb_ref[r0:r0 + C, :]
        ext_ref[s, 8:8 + C, B_DINNER + ng:B_CONV_DIM] = bc_ref[r0:r0 + C, :]

    def conv_silu(s, lo):
        acc = cw_ref[0:1, lo:lo + 128] * ext_ref[s, 5:5 + C, lo:lo + 128]
        for j in range(1, SHORT_CONV):
            acc = acc + cw_ref[j:j + 1, lo:lo + 128] * ext_ref[s, 5 + j:5 + j + C, lo:lo + 128]
        return _silu(acc + cb_ref[0:1, lo:lo + 128])

    def pick(arr, l0, mask):
        return jnp.where(mask, arr[:, l0:l0 + 1], arr[:, l0 + 1:l0 + 2])

    gates = []
    for s in range(NS):
        sm = sm_ref[s * C:(s + 1) * C, :]
        dt_all = _softplus(sm + par_ref[1:2, :])
        gc_all = _cumsum_rows(dt_all * (-jnp.exp(par_ref[0:1, :])), C)
        gl_all = gc_all[C - 1:C, :]
        gates.append((dt_all, gc_all, jnp.exp(gc_all), jnp.exp(gl_all), jnp.exp(gl_all - gc_all)))
    d_all = par_ref[2:3, :]

    sg = [(s, g) for s in range(NS) for g in range(B_GROUPS)]
    bgs = {k: conv_silu(k[0], B_DINNER + k[1] * B_DSTATE) for k in sg}
    cgs = {k: conv_silu(k[0], B_DINNER + ng + k[1] * B_DSTATE) for k in sg}
    scores = {k: _dot_nt(cgs[k], bgs[k]) for k in sg}
    insts = [(s, p) for s in range(NS) for p in range(B_PAIRS)]
    lanes = [LANE_BDT + 2 * p for _, p in insts]
    grp = [(s, p // pairs_per_group) for s, p in insts]
    xps = [conv_silu(s, p * 128) for s, p in insts]
    vs = [xp * pick(gates[s][0], l0, first) for xp, (s, _), l0 in zip(xps, insts, lanes)]
    vbs = [v.astype(bf16) for v in vs]
    sds = []
    for (s, _), l0, k, vb in zip(insts, lanes, grp, vbs):
        pair = []
        for l in (l0, l0 + 1):
            gcol = gates[s][1][:, l:l + 1]
            decay = jnp.exp(jnp.where(tri, gcol - _col_to_row(gcol, eye), NEG_BIG))
            pair.append(jnp.dot((scores[k] * decay).astype(bf16), vb, preferred_element_type=f32))
        sds.append(pair)
    sts = [so_ref[layer, s, p] for s, p in insts]
    outs = [jnp.where(first, sd[0], sd[1]) + _dot_nt(cgs[k], st) * pick(gates[s][2], l0, first)
            for sd, k, st, (s, _), l0 in zip(sds, grp, sts, insts, lanes)]
    for (s, p), l0, k, st, v in zip(insts, lanes, grp, sts, vs):
        so_ref[layer, s, p] = st * pick(gates[s][3], l0, first_row) + _dot_tn(v * pick(gates[s][4], l0, first), bgs[k])
    yvs = [(o + pick(d_all, l0, first_lane) * xp) * _silu(z_ref[s * C:(s + 1) * C, p * 128:(p + 1) * 128])
           for o, l0, xp, (s, p) in zip(outs, lanes, xps, insts)]
    ssq = {k: jnp.zeros((C, 1), f32) for k in sg}
    for k, yv in zip(grp, yvs):
        ssq[k] = ssq[k] + jnp.sum(yv * yv, axis=-1, keepdims=True)
    scale = {k: lax.rsqrt(v * (1.0 / (B_DINNER // B_GROUPS)) + EPS) for k, v in ssq.items()}
    for (s, p), k, yv in zip(insts, grp, yvs):
        yacc_ref[s * C:(s + 1) * C, p * 128:(p + 1) * 128] = yv * scale[k] * gn_ref[0:1, p * 128:(p + 1) * 128]
    y_ref[...] = yacc_ref[...].astype(y_ref.dtype)

    for s in range(NS):
        cso_ref[layer, s] = ext_ref[s, C + 5:C + 8, :]
        ext_ref[s, 0:8, :] = ext_ref[s, C:C + 8, :]


def _ssd(proj, cw, cb, par, gn, states_in, prev_states, **kw):
    ng = B_GROUPS * B_DSTATE
    return _mixer_call(
        _ssd_kernel, "ssd",
        [(MIX_W, OFF_BZ), (MIX_W, OFF_BX), (ng, OFF_BB), (ng, OFF_BC), (128, OFF_SMALL)],
        [(cw, (SHORT_CONV, B_CONV_DIM)), (cb, (1, B_CONV_DIM)), (par, (3, 128)), (gn, (1, B_DINNER))],
        states_in, [(B_PAIRS, 2 * B_HEADDIM, B_DSTATE), (SHORT_CONV - 1, B_CONV_DIM)],
        [pltpu.VMEM((kw["NS"], kw["C"] + 8, B_CONV_DIM), f32)], prev_states, proj=proj, **kw)


def _pack_ssm_state(s):
    lead = s.shape[:-3]
    return jnp.swapaxes(s, -1, -2).reshape(lead + (B_PAIRS, 2 * B_HEADDIM, B_DSTATE))


def _unpack_ssm_state(s):
    lead = s.shape[:-3]
    return jnp.swapaxes(s.reshape(lead + (B_HEADS, B_HEADDIM, B_DSTATE)), -1, -2)


_LOG_GAMMA = [float(np.log(np.float32(1.0) - np.float32(2.0) ** np.float32(-5.0 - h))) for h in range(C_HEADS)]


def _ret_kernel(pos0, C, NS, has_state, layer, *refs):
    q_ref, k_ref, v_ref, g_ref, inv_ref = refs[:5]
    n_in = 5
    if has_state:
        s0_ref = refs[5]
        n_in = 6
    if layer:
        prev_so_ref = refs[n_in]
        n_in += 1
    y_ref, so_ref, yacc_ref = refs[n_in:]
    c = pl.program_id(1)

    @pl.when(c == 0)
    def _():
        so_ref[layer] = s0_ref[0] if has_state else jnp.zeros(so_ref.shape[1:], f32)
        if layer:
            so_ref[0:layer] = prev_so_ref[...]

    half = C_DK // 2
    pos = (pos0 + c * C + lax.broadcasted_iota(jnp.int32, (C, half), 0)).astype(f32)
    ang = pos * inv_ref[...]
    cos = jnp.cos(ang)
    sin = jnp.sin(ang)

    ii = lax.broadcasted_iota(jnp.int32, (C, C), 0)
    jj = lax.broadcasted_iota(jnp.int32, (C, C), 1)
    tri = ii >= jj
    dij = (ii - jj).astype(f32)
    ipos = (lax.broadcasted_iota(jnp.int32, (C, 1), 0) + 1).astype(f32)

    for s in range(NS):
        r0 = s * C

        def rot(ref, h, r0=r0):
            t1 = ref[r0:r0 + C, h * C_DK:h * C_DK + half]
            t2 = ref[r0:r0 + C, h * C_DK + half:(h + 1) * C_DK]
            return jnp.concatenate([t1 * cos - t2 * sin, t1 * sin + t2 * cos], axis=-1)

        for h in range(C_HEADS):
            lg = _LOG_GAMMA[h]
            q = rot(q_ref, h)
            k = rot(k_ref, h) * (C_DK ** -0.5)
            v = v_ref[r0:r0 + C, h * C_DV:(h + 1) * C_DV]
            decay = jnp.exp(jnp.where(tri, dij * lg, NEG_BIG))
            egc = jnp.exp(ipos * lg)
            edl = jnp.exp((float(C) - ipos) * lg)
            egl = float(np.exp(np.float32(C * lg)))
            st = so_ref[layer, s, h]
            o = _dot(_dot_nt(q, k) * decay, v) + _dot(q, st) * egc
            so_ref[layer, s, h] = st * egl + _dot_tn(k, v * edl)
            o = o * lax.rsqrt(jnp.mean(o * o, axis=-1, keepdims=True) + EPS)
            yacc_ref[r0:r0 + C, h * C_DV:(h + 1) * C_DV] = o * _silu(g_ref[r0:r0 + C, h * C_DV:(h + 1) * C_DV])
    y_ref[...] = yacc_ref[...].astype(y_ref.dtype)


def _ret(proj, inv, states_in, prev_states, *, pos0, **kw):
    return _mixer_call(
        functools.partial(_ret_kernel, pos0), "ret",
        [(MIX_W, OFF_CQ), (MIX_W, OFF_CK), (MIX_W, OFF_CV), (MIX_W, OFF_CG)],
        [(inv, (1, C_DK // 2))],
        states_in, [(C_HEADS, C_DK, C_DV)], [], prev_states, proj=proj, **kw)


FFN_TN = 256


def _ffn_gu_prompt_kernel(tm, tiles_per_seq, layer, *refs):
    u_ref, wg_ref, wu_ref, cw_ref, cb_ref = refs[:5]
    n_in = 5
    if layer:
        prev_cso_ref = refs[5]
        n_in = 6
    a_ref, cso_ref, wgb_ref, wub_ref, ext_ref = refs[n_in:]
    i = pl.program_id(1)

    @pl.when(i == 0)
    def _():
        wgb_ref[...] = wg_ref[...].astype(bf16)
        wub_ref[...] = wu_ref[...].astype(bf16)

    @pl.when(i % tiles_per_seq == 0)
    def _():
        ext_ref[0:8, :] = jnp.zeros((8, FFN_TN), f32)

    u = u_ref[...]
    ext_ref[8:8 + tm, :] = jnp.dot(u, wgb_ref[...], preferred_element_type=f32)
    acc = cw_ref[0:1, :] * ext_ref[6:6 + tm, :]
    for j in range(1, FFN_CONV):
        acc = acc + cw_ref[j:j + 1, :] * ext_ref[6 + j:6 + j + tm, :]
    acc = _silu(acc + cb_ref[...])
    a_ref[...] = (acc * jnp.dot(u, wub_ref[...], preferred_element_type=f32)).astype(a_ref.dtype)
    cso_ref[layer, 0] = ext_ref[tm + 6:tm + 8, :]
    if layer:
        cso_ref[0:layer] = prev_cso_ref[...]
    ext_ref[0:8, :] = ext_ref[tm:tm + 8, :]


def _ffn_gu_sample_kernel(nb, L, layer, *refs):
    u_ref, wg_ref, wu_ref, cw_ref, cb_ref, cs_ref = refs[:6]
    n_in = 6
    if layer:
        prev_cso_ref = refs[6]
        n_in = 7
    a_ref, cso_ref, ext_ref = refs[n_in:]
    if layer:
        cso_ref[0:layer] = prev_cso_ref[...]
    u = u_ref[...]
    gate = jnp.dot(u, wg_ref[...].astype(bf16), preferred_element_type=f32)
    ext_ref[:, 0:8, :] = jnp.zeros((nb, 8, FFN_TN), f32)
    ext_ref[:, 6:8, :] = cs_ref[0]
    ext_ref[:, 8:8 + L, :] = gate.reshape(nb, L, FFN_TN)
    acc = cw_ref[0:1, :] * ext_ref[:, 6:6 + L, :]
    for j in range(1, FFN_CONV):
        acc = acc + cw_ref[j:j + 1, :] * ext_ref[:, 6 + j:6 + j + L, :]
    acc = _silu(acc + cb_ref[...])
    up = jnp.dot(u, wu_ref[...].astype(bf16), preferred_element_type=f32)
    a_ref[...] = (acc.reshape(nb * L, FFN_TN) * up).astype(a_ref.dtype)
    cso_ref[layer] = ext_ref[:, L + 6:L + 8, :]


def _ffn_gu(u2, wg, wu, cw, cb, cs, prev_state, *, layer, nb, L):
    d = u2.shape[1]
    nf = wg.shape[-1]
    tn = FFN_TN
    w_spec = pl.BlockSpec((None, d, tn), lambda j, i: (layer, 0, j))
    cw_spec = pl.BlockSpec((None, FFN_CONV, tn), lambda j, i: (layer, 0, j))
    cb_spec = pl.BlockSpec((None, 1, tn), lambda j, i: (layer, 0, j))
    args = [u2, wg, wu, cw, cb]
    if cs is None:
        tm = _tile(L, 1024, 16)
        tps = L // tm
        grid = (nf // tn, nb * tps)
        in_specs = [pl.BlockSpec((tm, d), lambda j, i: (i, 0)), w_spec, w_spec, cw_spec, cb_spec]
        out_specs = [pl.BlockSpec((tm, tn), lambda j, i: (i, j)),
                     pl.BlockSpec((layer + 1, 1, FFN_CONV - 1, tn), lambda j, i: (0, i // tps, 0, j))]
        prev_spec = pl.BlockSpec((layer, 1, FFN_CONV - 1, tn), lambda j, i: (0, i // tps, 0, j))
        body = functools.partial(_ffn_gu_prompt_kernel, tm, tps, layer)
        scratch = [pltpu.VMEM((d, tn), bf16), pltpu.VMEM((d, tn), bf16), pltpu.VMEM((tm + 8, tn), f32)]
        name = "ffn_gu"
    else:
        tm = nb * L
        assert L % 8 == 0
        grid = (nf // tn, 1)
        in_specs = [pl.BlockSpec((tm, d), lambda j, i: (0, 0)), w_spec, w_spec, cw_spec, cb_spec,
                    pl.BlockSpec((1, nb, FFN_CONV - 1, tn), lambda j, i: (layer, 0, 0, j))]
        out_specs = [pl.BlockSpec((tm, tn), lambda j, i: (0, j)),
                     pl.BlockSpec((layer + 1, nb, FFN_CONV - 1, tn), lambda j, i: (0, 0, 0, j))]
        prev_spec = pl.BlockSpec((layer, nb, FFN_CONV - 1, tn), lambda j, i: (0, 0, 0, j))
        args.append(cs)
        body = functools.partial(_ffn_gu_sample_kernel, nb, L, layer)
        scratch = [pltpu.VMEM((nb, L + 8, tn), f32)]
        name = "ffn_gu_state"
    if layer:
        in_specs.append(prev_spec)
        args.append(prev_state)
    return pl.pallas_call(
        body,
        grid=grid,
        in_specs=in_specs,
        out_specs=out_specs,
        out_shape=[jax.ShapeDtypeStruct((nb * L, nf), bf16),
                   jax.ShapeDtypeStruct((layer + 1, nb, FFN_CONV - 1, nf), f32)],
        scratch_shapes=scratch,
        compiler_params=_cparams("parallel", "arbitrary"),
        name=name,
    )(*args)


def _lane_row(vals, lane0):
    return jnp.zeros((128,), f32).at[lane0:lane0 + vals.shape[0]].set(vals.astype(f32))


def kernel(x_prompt, x_sample, state_gdn, state_gdn_conv, state_ssm, state_ssm_conv, state_ret, state_ffn_conv, norm_mix, w_in, gdn_conv_w, gdn_a_log, gdn_dt_bias, gdn_norm, ssm_conv_w, ssm_conv_b, ssm_a_log, ssm_dt_bias, ssm_d, ssm_norm, w_branch_a, w_branch_b, w_branch_c, w_out, norm_ffn, w_ffn_gate, w_ffn_up, ffn_conv_w, ffn_conv_b, w_ffn_down, norm_final):
    nbp, lp, d = x_prompt.shape
    nbs, ls, _ = x_sample.shape
    depth = w_in.shape[0]
    half = C_DK // 2
    inv = (ROPE_BASE ** (-jnp.arange(half, dtype=f32) / half)).reshape(1, half)
    ssm_in = _pack_ssm_state(state_ssm)
    w_in_t = jnp.swapaxes(w_in, 1, 2)
    wa_bf, wb_bf, wc_bf = w_branch_a.astype(bf16), w_branch_b.astype(bf16), w_branch_c.astype(bf16)
    wd_bf = w_ffn_down.astype(bf16)
    fcb = ffn_conv_b.reshape(depth, 1, D_FF)

    def run_group(x3, states, pos0, ns):
        nb, L, _ = x3.shape
        x = x3.reshape(nb * L, d)
        chunk = CHUNK if L % CHUNK == 0 else L
        st = [None] * 6

        def sin(*idx):
            return [] if states is None else [states[k] for k in idx]

        for l in range(depth):
            kw = dict(layer=l, nb=nb, L=L, C=chunk, NS=ns)
            u = _rmsnorm(x, norm_mix[l], bf16)
            proj = _in_proj(u, w_in_t, l)

            gdn_par = jnp.stack([_lane_row(gdn_a_log[l], LANE_AA), _lane_row(gdn_dt_bias[l], LANE_AA)])
            ya, st[0], st[1] = _gdn(proj, gdn_conv_w[l], gdn_par, gdn_norm[l].reshape(1, A_DV), sin(0, 1),
                                    [st[0], st[1]], **kw)
            ssd_par = jnp.stack([_lane_row(ssm_a_log[l], LANE_BDT), _lane_row(ssm_dt_bias[l], LANE_BDT),
                                 _lane_row(ssm_d[l], LANE_BDT)])
            yb, st[2], st[3] = _ssd(proj, ssm_conv_w[l], ssm_conv_b[l].reshape(1, B_CONV_DIM), ssd_par,
                                    ssm_norm[l].reshape(1, B_DINNER), sin(2, 3), [st[2], st[3]], **kw)
            yc, st[4] = _ret(proj, inv, sin(4), [st[4]], pos0=pos0, **kw)

            h = _merge(ya, yb, yc, wa_bf, wb_bf, wc_bf, proj, l)
            x = _matmul(h, w_out, res=x, layer=l, name="out_proj")

            u2 = _rmsnorm(x, norm_ffn[l], bf16)
            act, st[5] = _ffn_gu(u2, w_ffn_gate, w_ffn_up, ffn_conv_w, fcb, None if states is None else states[5],
                                 st[5], layer=l, nb=nb, L=L)
            x = _matmul(act, wd_bf, res=x, layer=l, tk=D_FF // 2, name="ffn_down")
        y = _rmsnorm(x, norm_final, f32).reshape(nb, L, d)
        st[2] = _unpack_ssm_state(st[2])
        return (y,) + tuple(st)

    out_p = run_group(x_prompt, None, 0, 1)
    out_s = run_group(x_sample, (state_gdn, state_gdn_conv, ssm_in, state_ssm_conv, state_ret, state_ffn_conv),
                      PAST_LEN, SAMPLE_SEQS_PER_STEP)
    return (out_p[0], out_s[0]) + out_p[1:] + out_s[1:]
```

```python
import functools

import numpy as np
import jax
import jax.numpy as jnp
from jax import lax
from jax.experimental import pallas as pl
from jax.experimental.pallas import tpu as pltpu

f32 = jnp.float32
bf16 = jnp.bfloat16

D_MODEL = 4096
A_HEADS, A_DK, A_DV = 16, 128, 128
A_QKV = A_HEADS * (2 * A_DK + A_DV)
B_HEADS, B_HEADDIM, B_GROUPS, B_DSTATE = 32, 64, 4, 128
B_DINNER = B_HEADS * B_HEADDIM
B_REP = B_HEADS // B_GROUPS
B_PAIRS = B_HEADS // 2
B_CONV_DIM = B_DINNER + 2 * B_GROUPS * B_DSTATE
C_HEADS, C_DK, C_DV = 8, 256, 256
SHORT_CONV = 4
FFN_CONV = 3
CHUNK = 64
D_FF = 11008
ROPE_BASE = 10000.0
EPS = 1e-6
PAST_LEN = 16384
MIX_W = 2048

OFF_QKV = 0
OFF_AZ = 6144
OFF_BZ = 8192
OFF_BX = 10240
OFF_CQ = 12288
OFF_CK = 14336
OFF_CV = 16384
OFF_CG = 18432
OFF_GATES = 20480
OFF_BB = 32768
OFF_BC = 33280
OFF_SMALL = 33792
NP = 34304
LANE_AB, LANE_AA, LANE_BDT = 0, 16, 32

VMEM_LIMIT = 56 * 1024 * 1024
NEG_BIG = -1e30
SAMPLE_SEQS_PER_STEP = 2


def _cparams(*sem):
    return pltpu.CompilerParams(dimension_semantics=sem, vmem_limit_bytes=VMEM_LIMIT)


def _tile(n, pref, mult):
    t = min(pref, n)
    t -= t % mult
    while t >= mult:
        if n % t == 0:
            return t
        t -= mult
    return n


def _sigmoid(x):
    return 1.0 / (1.0 + jnp.exp(-x))


def _silu(x):
    return x * _sigmoid(x)


def _softplus(x):
    return jnp.maximum(x, 0.0) + jnp.log(1.0 + jnp.exp(-jnp.abs(x)))


def _dot(a, b):
    return jnp.dot(a.astype(bf16), b.astype(bf16), preferred_element_type=f32)


def _dot_nt(a, b):
    return lax.dot_general(a.astype(bf16), b.astype(bf16), (((1,), (1,)), ((), ())), preferred_element_type=f32)


def _dot_tn(a, b):
    return lax.dot_general(a.astype(bf16), b.astype(bf16), (((0,), (0,)), ((), ())), preferred_element_type=f32)


def _split(a):
    hi = lax.bitcast_convert_type(lax.bitcast_convert_type(a, jnp.int32) & jnp.int32(-65536), f32)
    return hi, a - hi


def _prep_l(sa):
    hi, lo = sa
    if hi.shape[1] == 64:
        return (jnp.concatenate([hi, lo, hi], axis=1).astype(bf16),)
    return hi.astype(bf16), lo.astype(bf16)


def _prep_r(sb):
    hi, lo = sb
    if hi.shape[0] == 64:
        return (jnp.concatenate([hi, hi, lo], axis=0).astype(bf16),)
    return hi.astype(bf16), lo.astype(bf16)


def _mm3(l, r):
    d = functools.partial(jnp.dot, preferred_element_type=f32)
    if len(l) == 1:
        return d(l[0], r[0])
    return d(l[0], r[0]) + (d(l[1], r[0]) + d(l[0], r[1]))


def _cumsum_rows(x, n):
    row = lax.broadcasted_iota(jnp.int32, x.shape, 0)
    s = 1
    while s < n:
        x = x + jnp.where(row >= s, pltpu.roll(x, s, axis=0), 0.0)
        s *= 2
    return x


def _col_to_row(col, eye):
    return jnp.sum(jnp.where(eye, col, 0.0), axis=0, keepdims=True)


def _inv_unit_lower(lmats, ii, jj, n):
    eye = (ii == jj).astype(f32)
    blk = (ii >> 3) == (jj >> 3)
    ps = [jnp.where(blk, -l, 0.0) for l in lmats]
    ts = [eye + p for p in ps]
    sp = [_split(p) for p in ps]
    p2 = [_mm3(_prep_l(s), _prep_r(s)) for s in sp]
    sp2 = [_split(q) for q in p2]
    p2r = [_prep_r(s) for s in sp2]
    ts = [t + _mm3(_prep_l(_split(t)), r) for t, r in zip(ts, p2r)]
    p4 = [_mm3(_prep_l(s), r) for s, r in zip(sp2, p2r)]
    ts = [t + _mm3(_prep_l(_split(t)), _prep_r(_split(q))) for t, q in zip(ts, p4)]
    s = 8
    while s < n:
        sh = s.bit_length() - 1
        mask = ((ii >> (sh + 1)) == (jj >> (sh + 1))) & ((ii >> sh) != (jj >> sh))
        st = [_split(t) for t in ts]
        mids = [_mm3(_prep_l(_split(jnp.where(mask, l, 0.0))), _prep_r(s_)) for l, s_ in zip(lmats, st)]
        ts = [t - _mm3(_prep_l(s_), _prep_r(_split(m))) for t, s_, m in zip(ts, st, mids)]
        s *= 2
    return ts


def _rmsnorm_kernel(x_ref, g_ref, o_ref):
    x = x_ref[...]
    y = x * lax.rsqrt(jnp.mean(x * x, axis=-1, keepdims=True) + EPS) * g_ref[...]
    o_ref[...] = y.astype(o_ref.dtype)


def _rmsnorm(x, gain, out_dtype):
    t, d = x.shape
    tm = _tile(t, 256, 16)
    return pl.pallas_call(
        _rmsnorm_kernel,
        grid=(t // tm,),
        in_specs=[pl.BlockSpec((tm, d), lambda i: (i, 0)), pl.BlockSpec((1, d), lambda i: (0, 0))],
        out_specs=pl.BlockSpec((tm, d), lambda i: (i, 0)),
        out_shape=jax.ShapeDtypeStruct((t, d), out_dtype),
        compiler_params=_cparams("parallel"),
        name="rmsnorm",
    )(x, gain.reshape(1, d))


def _mm_kernel(a_ref, b_ref, r_ref, o_ref):
    o_ref[...] = r_ref[...] + jnp.dot(a_ref[...], b_ref[...].astype(bf16), preferred_element_type=f32)


def _matmul(a, b, res, *, layer, tm, tn, name):
    m, kdim = a.shape
    n = b.shape[-1]
    tm = _tile(m, tm, 16)
    tn = _tile(n, tn, 128)
    return pl.pallas_call(
        _mm_kernel,
        grid=(m // tm, n // tn),
        in_specs=[pl.BlockSpec((tm, kdim), lambda i, j: (i, 0)),
                  pl.BlockSpec((None, kdim, tn), lambda i, j: (layer, 0, j)),
                  pl.BlockSpec((tm, tn), lambda i, j: (i, j))],
        out_specs=pl.BlockSpec((tm, tn), lambda i, j: (i, j)),
        out_shape=jax.ShapeDtypeStruct((m, n), f32),
        compiler_params=_cparams("parallel", "parallel"),
        name=name,
    )(a, b, res)


IN_PROJ_TN = 512


def _in_proj_kernel(n_main, offs_ref, u_ref, wt_ref, ws_ref, o_ref):
    j = pl.program_id(1)

    @pl.when(j < n_main)
    def _():
        o_ref[...] = _dot_nt(u_ref[...], wt_ref[0])

    @pl.when(j == n_main)
    def _():
        o_ref[...] = _dot_nt(u_ref[...], ws_ref[...])


def _in_proj(u, w_in_t, layer):
    o_ab = A_QKV + A_HEADS * A_DV
    o_bz = o_ab + 2 * A_HEADS
    o_bbc = o_bz + 2 * B_DINNER
    o_dt = o_bbc + 2 * B_GROUPS * B_DSTATE
    o_cq = o_dt + B_HEADS
    total = w_in_t.shape[1]
    tn = IN_PROJ_TN
    runs = [(OFF_QKV, 0, o_ab), (OFF_BZ, o_bz, o_bbc - o_bz), (OFF_CQ, o_cq, total - o_cq), (OFF_BB, o_bbc, o_dt - o_bbc)]
    offs = []
    for col0, src, width in runs:
        assert col0 == len(offs) * tn and width % tn == 0 and src % 32 == 0
        offs += [src + t * tn for t in range(width // tn)]
    assert len(offs) * tn == OFF_SMALL and NP == OFF_SMALL + tn
    n_main = len(offs)
    offs.append(0)
    m, kdim = u.shape
    tm = _tile(m, 1024, 16)
    small = jnp.concatenate([w_in_t[layer, o_ab:o_bz, :], w_in_t[layer, o_dt:o_cq, :],
                             jnp.zeros((tn - (o_bz - o_ab) - (o_cq - o_dt), kdim), w_in_t.dtype)], axis=0)
    grid_spec = pltpu.PrefetchScalarGridSpec(
        num_scalar_prefetch=1, grid=(m // tm, n_main + 1),
        in_specs=[pl.BlockSpec((pl.Element(tm), pl.Element(kdim)), lambda i, j, o: (i * tm, 0)),
                  pl.BlockSpec((pl.Element(1), pl.Element(tn), pl.Element(kdim)),
                               lambda i, j, o: (layer, pl.multiple_of(o[j], 32), 0)),
                  pl.BlockSpec((pl.Element(tn), pl.Element(kdim)), lambda i, j, o: (0, 0))],
        out_specs=pl.BlockSpec((tm, tn), lambda i, j, o: (i, j)))
    return pl.pallas_call(
        functools.partial(_in_proj_kernel, n_main), grid_spec=grid_spec,
        out_shape=jax.ShapeDtypeStruct((m, NP), f32),
        compiler_params=_cparams("parallel", "arbitrary"),
        name="in_proj",
    )(jnp.asarray(offs, jnp.int32), u, w_in_t, small)


def _merge_kernel(ya_ref, yb_ref, yc_ref, wa_ref, wb_ref, wc_ref, ga_ref, gb_ref, gc_ref, o_ref):
    d = functools.partial(jnp.dot, preferred_element_type=f32)
    h = _sigmoid(ga_ref[...]) * d(ya_ref[...], wa_ref[...])
    h = h + _sigmoid(gb_ref[...]) * d(yb_ref[...], wb_ref[...])
    h = h + _sigmoid(gc_ref[...]) * d(yc_ref[...], wc_ref[...])
    o_ref[...] = h.astype(o_ref.dtype)


def _merge(ya, yb, yc, wa, wb, wc, proj, layer):
    t, kd = ya.shape
    n = wa.shape[-1]
    tm = _tile(t, 1024, 16)
    tn = 512
    g0 = OFF_GATES // tn
    gstep = n // tn
    yspec = pl.BlockSpec((tm, kd), lambda i, j: (i, 0))
    wspec = pl.BlockSpec((None, kd, tn), lambda i, j: (layer, 0, j))
    gspecs = [pl.BlockSpec((tm, tn), functools.partial(lambda i, j, o: (i, o + j), o=g0 + r * gstep)) for r in range(3)]
    return pl.pallas_call(
        _merge_kernel,
        grid=(t // tm, n // tn),
        in_specs=[yspec, yspec, yspec, wspec, wspec, wspec] + gspecs,
        out_specs=pl.BlockSpec((tm, tn), lambda i, j: (i, j)),
        out_shape=jax.ShapeDtypeStruct((t, n), bf16),
        compiler_params=_cparams("parallel", "parallel"),
        name="merge",
    )(ya, yb, yc, wa, wb, wc, proj, proj, proj)


def _mixer_call(kernel_fn, name, proj_cols, consts, states_in, state_shapes, scratch, prev_states, *,
                proj, layer, depth, nb, L, C, NS):
    nch = L // C
    assert NS == 1 or nch == 1

    def rowmap(col):
        return lambda i, c: (i * nch + c, col)

    def stspec(nlayers, shp, l0):
        return pl.BlockSpec((nlayers, NS) + shp, functools.partial(lambda i, c, nd: (l0, i) + (0,) * nd, nd=len(shp)))

    rows = NS * C
    in_specs = [pl.BlockSpec((rows, w), rowmap(off // w)) for w, off in proj_cols]
    args = [proj] * len(proj_cols)
    for arr, shp in consts:
        in_specs.append(pl.BlockSpec(shp, functools.partial(lambda i, c, nd: (0,) * nd, nd=len(shp))))
        args.append(arr)
    for arr, shp in zip(states_in, state_shapes):
        in_specs.append(stspec(1, shp, layer))
        args.append(arr)
    aliases = {}
    if layer:
        for k, arr in enumerate(prev_states):
            aliases[len(args)] = 1 + k
            in_specs.append(pl.BlockSpec(memory_space=pl.ANY))
            args.append(arr)
        st_specs = [stspec(1, shp, layer) for shp in state_shapes]
    else:
        st_specs = [stspec(depth, shp, 0) for shp in state_shapes]
    return pl.pallas_call(
        functools.partial(kernel_fn, C, NS, bool(states_in), len(aliases)),
        grid=(nb // NS, nch),
        in_specs=in_specs,
        out_specs=[pl.BlockSpec((rows, MIX_W), rowmap(0))] + st_specs,
        out_shape=[jax.ShapeDtypeStruct((nb * L, MIX_W), bf16)]
        + [jax.ShapeDtypeStruct((depth, nb) + shp, f32) for shp in state_shapes],
        scratch_shapes=scratch + [pltpu.VMEM((rows, MIX_W), f32)],
        input_output_aliases=aliases,
        compiler_params=_cparams("parallel", "arbitrary"),
        name=name + ("_state" if states_in else ""),
    )(*args)


def _fill_layer_slots(c, *state_refs):
    @pl.when(c == pl.num_programs(1) - 1)
    def _():
        for ref in state_refs:
            for l in range(1, ref.shape[0]):
                ref[l] = ref[0]


def _gdn_kernel(C, NS, has_state, n_alias, *refs):
    qkv_ref, z_ref, sm_ref, cw_ref, par_ref, gn_ref = refs[:6]
    n_in = 6
    if has_state:
        s0_ref, cs_ref = refs[6:8]
        n_in = 8
    y_ref, so_ref, cso_ref, ext_ref, yacc_ref = refs[n_in + n_alias:]
    c = pl.program_id(1)

    @pl.when(c == 0)
    def _():
        for s in range(NS):
            ext_ref[s, 0:8, :] = jnp.zeros((8, A_QKV), f32)
            if has_state:
                ext_ref[s, 5:8, :] = cs_ref[0, s]
        so_ref[0] = s0_ref[0] if has_state else jnp.zeros(so_ref.shape[1:], f32)

    for s in range(NS):
        ext_ref[s, 8:8 + C, :] = qkv_ref[s * C:(s + 1) * C, :]

    def conv_silu(s, lo):
        acc = cw_ref[0:1, lo:lo + 128] * ext_ref[s, 5:5 + C, lo:lo + 128]
        for j in range(1, SHORT_CONV):
            acc = acc + cw_ref[j:j + 1, lo:lo + 128] * ext_ref[s, 5 + j:5 + j + C, lo:lo + 128]
        return _silu(acc)

    gates = []
    for s in range(NS):
        sm = sm_ref[s * C:(s + 1) * C, :]
        beta_all = _sigmoid(sm)
        g_all = -jnp.exp(par_ref[0:1, :]) * _softplus(sm + par_ref[1:2, :])
        gc_all = _cumsum_rows(g_all, C)
        gl_all = gc_all[C - 1:C, :]
        gates.append((beta_all, gc_all, jnp.exp(gc_all), jnp.exp(gl_all), jnp.exp(gl_all - gc_all)))

    ii = lax.broadcasted_iota(jnp.int32, (C, C), 0)
    jj = lax.broadcasted_iota(jnp.int32, (C, C), 1)
    eye = ii == jj
    tri = ii >= jj
    strict = ii > jj
    gain = gn_ref[...]

    def col(s, which, lane):
        return gates[s][which][:, lane:lane + 1]

    insts = [(s, h) for h in range(A_HEADS) for s in range(NS)]
    qs = [conv_silu(s, h * A_DK) for s, h in insts]
    qs = [q * lax.rsqrt(jnp.sum(q * q, axis=-1, keepdims=True) + EPS) * (A_DK ** -0.5) for q in qs]
    ks = [conv_silu(s, A_HEADS * A_DK + h * A_DK) for s, h in insts]
    ks = [k * lax.rsqrt(jnp.sum(k * k, axis=-1, keepdims=True) + EPS) for k in ks]
    vs = [conv_silu(s, 2 * A_HEADS * A_DK + h * A_DV) for s, h in insts]
    betas = [col(s, 0, LANE_AB + h) for s, h in insts]
    gcols = [col(s, 1, LANE_AA + h) for s, h in insts]
    egcs = [col(s, 2, LANE_AA + h) for s, h in insts]
    decays = [jnp.exp(jnp.where(tri, g - _col_to_row(g, eye), NEG_BIG)) for g in gcols]
    kbs = [k * b for k, b in zip(ks, betas)]
    lmats = [jnp.where(strict, _dot_nt(kb, k) * d, 0.0) for kb, k, d in zip(kbs, ks, decays)]
    rhss = [jnp.concatenate([v * b, kb * e], axis=-1) for v, b, kb, e in zip(vs, betas, kbs, egcs)]
    tinv = _inv_unit_lower(lmats, ii, jj, C)
    sols = [_mm3(_prep_l(_split(t)), _prep_r(_split(r))) for t, r in zip(tinv, rhss)]
    sts = [so_ref[0, s, h] for s, h in insts]
    vnews = [sol[:, :A_DV] - _dot(sol[:, A_DV:], st) for sol, st in zip(sols, sts)]
    qks = [jnp.where(tri, _dot_nt(q, k) * d, 0.0) for q, k, d in zip(qs, ks, decays)]
    outs = [_dot(q * e, st) + _dot(qk, vn) for q, e, st, qk, vn in zip(qs, egcs, sts, qks, vnews)]
    for (s, h), k, st, vn in zip(insts, ks, sts, vnews):
        so_ref[0, s, h] = st * col(s, 3, LANE_AA + h) + _dot_tn(k * col(s, 4, LANE_AA + h), vn)
    for (s, h), o in zip(insts, outs):
        o = o * lax.rsqrt(jnp.mean(o * o, axis=-1, keepdims=True) + EPS) * gain
        yacc_ref[s * C:(s + 1) * C, h * A_DV:(h + 1) * A_DV] = o * _silu(
            z_ref[s * C:(s + 1) * C, h * A_DV:(h + 1) * A_DV])
    y_ref[...] = yacc_ref[...].astype(y_ref.dtype)

    for s in range(NS):
        cso_ref[0, s] = ext_ref[s, C + 5:C + 8, :]
        ext_ref[s, 0:8, :] = ext_ref[s, C:C + 8, :]
    _fill_layer_slots(c, so_ref, cso_ref)


def _gdn(proj, cw, par, gn, states_in, prev_states, **kw):
    return _mixer_call(
        _gdn_kernel, "gdn",
        [(A_QKV, OFF_QKV), (MIX_W, OFF_AZ), (128, OFF_SMALL)],
        [(cw, (SHORT_CONV, A_QKV)), (par, (2, 128)), (gn, (1, A_DV))],
        states_in, [(A_HEADS, A_DK, A_DV), (SHORT_CONV - 1, A_QKV)],
        [pltpu.VMEM((kw["NS"], kw["C"] + 8, A_QKV), f32)], prev_states, proj=proj, **kw)


def _ssd_kernel(C, NS, has_state, n_alias, *refs):
    z_ref, x_ref, bb_ref, bc_ref, sm_ref, cw_ref, cb_ref, par_ref, gn_ref = refs[:9]
    n_in = 9
    if has_state:
        s0_ref, cs_ref = refs[9:11]
        n_in = 11
    y_ref, so_ref, cso_ref, ext_ref, yacc_ref = refs[n_in + n_alias:]
    c = pl.program_id(1)
    ng = B_GROUPS * B_DSTATE

    @pl.when(c == 0)
    def _():
        for s in range(NS):
            ext_ref[s, 0:8, :] = jnp.zeros((8, B_CONV_DIM), f32)
            if has_state:
                ext_ref[s, 5:8, :] = cs_ref[0, s]
        so_ref[0] = s0_ref[0] if has_state else jnp.zeros(so_ref.shape[1:], f32)

    ii = lax.broadcasted_iota(jnp.int32, (C, C), 0)
    jj = lax.broadcasted_iota(jnp.int32, (C, C), 1)
    eye = ii == jj
    tri = ii >= jj
    first = lax.broadcasted_iota(jnp.int32, (C, 128), 1) < B_HEADDIM
    first_lane = lax.broadcasted_iota(jnp.int32, (1, 128), 1) < B_HEADDIM
    first_row = lax.broadcasted_iota(jnp.int32, (2 * B_HEADDIM, 1), 0) < B_HEADDIM
    pairs_per_group = B_REP // 2

    for s in range(NS):
        r0 = s * C
        ext_ref[s, 8:8 + C, 0:B_DINNER] = x_ref[r0:r0 + C, :]
        ext_ref[s, 8:8 + C, B_DINNER:B_DINNER + ng] = bb_ref[r0:r0 + C, :]
        ext_ref[s, 8:8 + C, B_DINNER + ng:B_CONV_DIM] = bc_ref[r0:r0 + C, :]

    def conv_silu(s, lo):
        acc = cw_ref[0:1, lo:lo + 128] * ext_ref[s, 5:5 + C, lo:lo + 128]
        for j in range(1, SHORT_CONV):
            acc = acc + cw_ref[j:j + 1, lo:lo + 128] * ext_ref[s, 5 + j:5 + j + C, lo:lo + 128]
        return _silu(acc + cb_ref[0:1, lo:lo + 128])

    def pick(arr, l0, mask):
        return jnp.where(mask, arr[:, l0:l0 + 1], arr[:, l0 + 1:l0 + 2])

    gates = []
    for s in range(NS):
        sm = sm_ref[s * C:(s + 1) * C, :]
        dt_all = _softplus(sm + par_ref[1:2, :])
        gc_all = _cumsum_rows(dt_all * (-jnp.exp(par_ref[0:1, :])), C)
        gl_all = gc_all[C - 1:C, :]
        gates.append((dt_all, gc_all, jnp.exp(gc_all), jnp.exp(gl_all), jnp.exp(gl_all - gc_all)))
    d_all = par_ref[2:3, :]

    sg = [(s, g) for s in range(NS) for g in range(B_GROUPS)]
    bgs = {k: conv_silu(k[0], B_DINNER + k[1] * B_DSTATE) for k in sg}
    cgs = {k: conv_silu(k[0], B_DINNER + ng + k[1] * B_DSTATE) for k in sg}
    scores = {k: _dot_nt(cgs[k], bgs[k]) for k in sg}
    insts = [(s, p) for s in range(NS) for p in range(B_PAIRS)]
    lanes = [LANE_BDT + 2 * p for _, p in insts]
    grp = [(s, p // pairs_per_group) for s, p in insts]
    xps = [conv_silu(s, p * 128) for s, p in insts]
    vs = [xp * pick(gates[s][0], l0, first) for xp, (s, _), l0 in zip(xps, insts, lanes)]
    vbs = [v.astype(bf16) for v in vs]
    sds = []
    for (s, _), l0, k, vb in zip(insts, lanes, grp, vbs):
        pair = []
        for l in (l0, l0 + 1):
            gcol = gates[s][1][:, l:l + 1]
            decay = jnp.exp(jnp.where(tri, gcol - _col_to_row(gcol, eye), NEG_BIG))
            pair.append(jnp.dot((scores[k] * decay).astype(bf16), vb, preferred_element_type=f32))
        sds.append(pair)
    sts = [so_ref[0, s, p] for s, p in insts]
    outs = [jnp.where(first, sd[0], sd[1]) + _dot_nt(cgs[k], st) * pick(gates[s][2], l0, first)
            for sd, k, st, (s, _), l0 in zip(sds, grp, sts, insts, lanes)]
    for (s, p), l0, k, st, v in zip(insts, lanes, grp, sts, vs):
        so_ref[0, s, p] = st * pick(gates[s][3], l0, first_row) + _dot_tn(v * pick(gates[s][4], l0, first), bgs[k])
    yvs = [(o + pick(d_all, l0, first_lane) * xp) * _silu(z_ref[s * C:(s + 1) * C, p * 128:(p + 1) * 128])
           for o, l0, xp, (s, p) in zip(outs, lanes, xps, insts)]
    ssq = {k: jnp.zeros((C, 1), f32) for k in sg}
    for k, yv in zip(grp, yvs):
        ssq[k] = ssq[k] + jnp.sum(yv * yv, axis=-1, keepdims=True)
    scale = {k: lax.rsqrt(v * (1.0 / (B_DINNER // B_GROUPS)) + EPS) for k, v in ssq.items()}
    for (s, p), k, yv in zip(insts, grp, yvs):
        yacc_ref[s * C:(s + 1) * C, p * 128:(p + 1) * 128] = yv * scale[k] * gn_ref[0:1, p * 128:(p + 1) * 128]
    y_ref[...] = yacc_ref[...].astype(y_ref.dtype)

    for s in range(NS):
        cso_ref[0, s] = ext_ref[s, C + 5:C + 8, :]
        ext_ref[s, 0:8, :] = ext_ref[s, C:C + 8, :]
    _fill_layer_slots(c, so_ref, cso_ref)


def _ssd(proj, cw, cb, par, gn, states_in, prev_states, **kw):
    ng = B_GROUPS * B_DSTATE
    return _mixer_call(
        _ssd_kernel, "ssd",
        [(MIX_W, OFF_BZ), (MIX_W, OFF_BX), (ng, OFF_BB), (ng, OFF_BC), (128, OFF_SMALL)],
        [(cw, (SHORT_CONV, B_CONV_DIM)), (cb, (1, B_CONV_DIM)), (par, (3, 128)), (gn, (1, B_DINNER))],
        states_in, [(B_PAIRS, 2 * B_HEADDIM, B_DSTATE), (SHORT_CONV - 1, B_CONV_DIM)],
        [pltpu.VMEM((kw["NS"], kw["C"] + 8, B_CONV_DIM), f32)], prev_states, proj=proj, **kw)


def _pack_ssm_state(s):
    lead = s.shape[:-3]
    return jnp.swapaxes(s, -1, -2).reshape(lead + (B_PAIRS, 2 * B_HEADDIM, B_DSTATE))


def _unpack_ssm_state(s):
    lead = s.shape[:-3]
    return jnp.swapaxes(s.reshape(lead + (B_HEADS, B_HEADDIM, B_DSTATE)), -1, -2)


_LOG_GAMMA = [float(np.log(np.float32(1.0) - np.float32(2.0) ** np.float32(-5.0 - h))) for h in range(C_HEADS)]


def _ret_kernel(pos0, C, NS, has_state, n_alias, *refs):
    q_ref, k_ref, v_ref, g_ref, inv_ref = refs[:5]
    n_in = 5
    if has_state:
        s0_ref = refs[5]
        n_in = 6
    y_ref, so_ref, yacc_ref = refs[n_in + n_alias:]
    c = pl.program_id(1)

    @pl.when(c == 0)
    def _():
        so_ref[0] = s0_ref[0] if has_state else jnp.zeros(so_ref.shape[1:], f32)

    half = C_DK // 2
    pos = (pos0 + c * C + lax.broadcasted_iota(jnp.int32, (C, half), 0)).astype(f32)
    ang = pos * inv_ref[...]
    cos = jnp.cos(ang)
    sin = jnp.sin(ang)

    ii = lax.broadcasted_iota(jnp.int32, (C, C), 0)
    jj = lax.broadcasted_iota(jnp.int32, (C, C), 1)
    tri = ii >= jj
    dij = (ii - jj).astype(f32)
    ipos = (lax.broadcasted_iota(jnp.int32, (C, 1), 0) + 1).astype(f32)

    for s in range(NS):
        r0 = s * C

        def rot(ref, h, r0=r0):
            t1 = ref[r0:r0 + C, h * C_DK:h * C_DK + half]
            t2 = ref[r0:r0 + C, h * C_DK + half:(h + 1) * C_DK]
            return jnp.concatenate([t1 * cos - t2 * sin, t1 * sin + t2 * cos], axis=-1)

        for h in range(C_HEADS):
            lg = _LOG_GAMMA[h]
            q = rot(q_ref, h)
            k = rot(k_ref, h) * (C_DK ** -0.5)
            v = v_ref[r0:r0 + C, h * C_DV:(h + 1) * C_DV]
            decay = jnp.exp(jnp.where(tri, dij * lg, NEG_BIG))
            egc = jnp.exp(ipos * lg)
            edl = jnp.exp((float(C) - ipos) * lg)
            egl = float(np.exp(np.float32(C * lg)))
            st = so_ref[0, s, h]
            o = _dot(_dot_nt(q, k) * decay, v) + _dot(q, st) * egc
            so_ref[0, s, h] = st * egl + _dot_tn(k, v * edl)
            o = o * lax.rsqrt(jnp.mean(o * o, axis=-1, keepdims=True) + EPS)
            yacc_ref[r0:r0 + C, h * C_DV:(h + 1) * C_DV] = o * _silu(g_ref[r0:r0 + C, h * C_DV:(h + 1) * C_DV])
    y_ref[...] = yacc_ref[...].astype(y_ref.dtype)
    _fill_layer_slots(c, so_ref)


def _ret(proj, inv, states_in, prev_states, *, pos0, **kw):
    return _mixer_call(
        functools.partial(_ret_kernel, pos0), "ret",
        [(MIX_W, OFF_CQ), (MIX_W, OFF_CK), (MIX_W, OFF_CV), (MIX_W, OFF_CG)],
        [(inv, (1, C_DK // 2))],
        states_in, [(C_HEADS, C_DK, C_DV)], [], prev_states, proj=proj, **kw)


FFN_TN = 256


def _ffn_gu_prompt_kernel(tm, tiles_per_seq, layer, *refs):
    u_ref, wg_ref, wu_ref, cw_ref, cb_ref = refs[:5]
    n_in = 5
    if layer:
        prev_cso_ref = refs[5]
        n_in = 6
    a_ref, cso_ref, wgb_ref, wub_ref, ext_ref = refs[n_in:]
    i = pl.program_id(1)

    @pl.when(i == 0)
    def _():
        wgb_ref[...] = wg_ref[...].astype(bf16)
        wub_ref[...] = wu_ref[...].astype(bf16)

    @pl.when(i % tiles_per_seq == 0)
    def _():
        ext_ref[0:8, :] = jnp.zeros((8, FFN_TN), f32)

    u = u_ref[...]
    ext_ref[8:8 + tm, :] = jnp.dot(u, wgb_ref[...], preferred_element_type=f32)
    acc = cw_ref[0:1, :] * ext_ref[6:6 + tm, :]
    for j in range(1, FFN_CONV):
        acc = acc + cw_ref[j:j + 1, :] * ext_ref[6 + j:6 + j + tm, :]
    acc = _silu(acc + cb_ref[...])
    a_ref[...] = (acc * jnp.dot(u, wub_ref[...], preferred_element_type=f32)).astype(a_ref.dtype)
    cso_ref[layer, 0] = ext_ref[tm + 6:tm + 8, :]
    if layer:
        cso_ref[0:layer] = prev_cso_ref[...]
    ext_ref[0:8, :] = ext_ref[tm:tm + 8, :]


def _ffn_gu_sample_kernel(nb, L, layer, *refs):
    u_ref, wg_ref, wu_ref, cw_ref, cb_ref, cs_ref = refs[:6]
    n_in = 6
    if layer:
        prev_cso_ref = refs[6]
        n_in = 7
    a_ref, cso_ref, ext_ref = refs[n_in:]
    if layer:
        cso_ref[0:layer] = prev_cso_ref[...]
    u = u_ref[...]
    gate = jnp.dot(u, wg_ref[...].astype(bf16), preferred_element_type=f32)
    ext_ref[:, 0:8, :] = jnp.zeros((nb, 8, FFN_TN), f32)
    ext_ref[:, 6:8, :] = cs_ref[0]
    ext_ref[:, 8:8 + L, :] = gate.reshape(nb, L, FFN_TN)
    acc = cw_ref[0:1, :] * ext_ref[:, 6:6 + L, :]
    for j in range(1, FFN_CONV):
        acc = acc + cw_ref[j:j + 1, :] * ext_ref[:, 6 + j:6 + j + L, :]
    acc = _silu(acc + cb_ref[...])
    up = jnp.dot(u, wu_ref[...].astype(bf16), preferred_element_type=f32)
    a_ref[...] = (acc.reshape(nb * L, FFN_TN) * up).astype(a_ref.dtype)
    cso_ref[layer] = ext_ref[:, L + 6:L + 8, :]


def _ffn_gu(u2, wg, wu, cw, cb, cs, prev_state, *, layer, nb, L):
    d = u2.shape[1]
    nf = wg.shape[-1]
    tn = FFN_TN
    w_spec = pl.BlockSpec((None, d, tn), lambda j, i: (layer, 0, j))
    cw_spec = pl.BlockSpec((None, FFN_CONV, tn), lambda j, i: (layer, 0, j))
    cb_spec = pl.BlockSpec((None, 1, tn), lambda j, i: (layer, 0, j))
    args = [u2, wg, wu, cw, cb]
    if cs is None:
        tm = _tile(L, 1024, 16)
        tps = L // tm
        grid = (nf // tn, nb * tps)
        in_specs = [pl.BlockSpec((tm, d), lambda j, i: (i, 0)), w_spec, w_spec, cw_spec, cb_spec]
        out_specs = [pl.BlockSpec((tm, tn), lambda j, i: (i, j)),
                     pl.BlockSpec((layer + 1, 1, FFN_CONV - 1, tn), lambda j, i: (0, i // tps, 0, j))]
        prev_spec = pl.BlockSpec((layer, 1, FFN_CONV - 1, tn), lambda j, i: (0, i // tps, 0, j))
        body = functools.partial(_ffn_gu_prompt_kernel, tm, tps, layer)
        scratch = [pltpu.VMEM((d, tn), bf16), pltpu.VMEM((d, tn), bf16), pltpu.VMEM((tm + 8, tn), f32)]
        name = "ffn_gu"
    else:
        tm = nb * L
        assert L % 8 == 0
        grid = (nf // tn, 1)
        in_specs = [pl.BlockSpec((tm, d), lambda j, i: (0, 0)), w_spec, w_spec, cw_spec, cb_spec,
                    pl.BlockSpec((1, nb, FFN_CONV - 1, tn), lambda j, i: (layer, 0, 0, j))]
        out_specs = [pl.BlockSpec((tm, tn), lambda j, i: (0, j)),
                     pl.BlockSpec((layer + 1, nb, FFN_CONV - 1, tn), lambda j, i: (0, 0, 0, j))]
        prev_spec = pl.BlockSpec((layer, nb, FFN_CONV - 1, tn), lambda j, i: (0, 0, 0, j))
        args.append(cs)
        body = functools.partial(_ffn_gu_sample_kernel, nb, L, layer)
        scratch = [pltpu.VMEM((nb, L + 8, tn), f32)]
        name = "ffn_gu_state"
    if layer:
        in_specs.append(prev_spec)
        args.append(prev_state)
    return pl.pallas_call(
        body,
        grid=grid,
        in_specs=in_specs,
        out_specs=out_specs,
        out_shape=[jax.ShapeDtypeStruct((nb * L, nf), bf16),
                   jax.ShapeDtypeStruct((layer + 1, nb, FFN_CONV - 1, nf), f32)],
        scratch_shapes=scratch,
        compiler_params=_cparams("parallel", "arbitrary"),
        name=name,
    )(*args)


def _lane_row(vals, lane0):
    return jnp.zeros((128,), f32).at[lane0:lane0 + vals.shape[0]].set(vals.astype(f32))


def kernel(x_prompt, x_sample, state_gdn, state_gdn_conv, state_ssm, state_ssm_conv, state_ret, state_ffn_conv, norm_mix, w_in, gdn_conv_w, gdn_a_log, gdn_dt_bias, gdn_norm, ssm_conv_w, ssm_conv_b, ssm_a_log, ssm_dt_bias, ssm_d, ssm_norm, w_branch_a, w_branch_b, w_branch_c, w_out, norm_ffn, w_ffn_gate, w_ffn_up, ffn_conv_w, ffn_conv_b, w_ffn_down, norm_final):
    nbp, lp, d = x_prompt.shape
    nbs, ls, _ = x_sample.shape
    depth = w_in.shape[0]
    half = C_DK // 2
    inv = (ROPE_BASE ** (-jnp.arange(half, dtype=f32) / half)).reshape(1, half)
    ssm_in = _pack_ssm_state(state_ssm)
    w_in_t = jnp.swapaxes(w_in, 1, 2)
    wa_bf, wb_bf, wc_bf = w_branch_a.astype(bf16), w_branch_b.astype(bf16), w_branch_c.astype(bf16)
    wd_bf = w_ffn_down.astype(bf16)
    fcb = ffn_conv_b.reshape(depth, 1, D_FF)

    def run_group(x3, states, pos0, ns):
        nb, L, _ = x3.shape
        x = x3.reshape(nb * L, d)
        chunk = CHUNK if L % CHUNK == 0 else L
        st = [None] * 6

        def sin(*idx):
            return [] if states is None else [states[k] for k in idx]

        for l in range(depth):
            kw = dict(layer=l, depth=depth, nb=nb, L=L, C=chunk, NS=ns)
            u = _rmsnorm(x, norm_mix[l], bf16)
            proj = _in_proj(u, w_in_t, l)

            gdn_par = jnp.stack([_lane_row(gdn_a_log[l], LANE_AA), _lane_row(gdn_dt_bias[l], LANE_AA)])
            ya, st[0], st[1] = _gdn(proj, gdn_conv_w[l], gdn_par, gdn_norm[l].reshape(1, A_DV), sin(0, 1),
                                    [st[0], st[1]], **kw)
            ssd_par = jnp.stack([_lane_row(ssm_a_log[l], LANE_BDT), _lane_row(ssm_dt_bias[l], LANE_BDT),
                                 _lane_row(ssm_d[l], LANE_BDT)])
            yb, st[2], st[3] = _ssd(proj, ssm_conv_w[l], ssm_conv_b[l].reshape(1, B_CONV_DIM), ssd_par,
                                    ssm_norm[l].reshape(1, B_DINNER), sin(2, 3), [st[2], st[3]], **kw)
            yc, st[4] = _ret(proj, inv, sin(4), [st[4]], pos0=pos0, **kw)

            h = _merge(ya, yb, yc, wa_bf, wb_bf, wc_bf, proj, l)
            x = _matmul(h, w_out, x, layer=l, tm=1024, tn=512, name="out_proj")

            u2 = _rmsnorm(x, norm_ffn[l], bf16)
            act, st[5] = _ffn_gu(u2, w_ffn_gate, w_ffn_up, ffn_conv_w, fcb, None if states is None else states[5],
                                 st[5], layer=l, nb=nb, L=L)
            x = _matmul(act, wd_bf, x, layer=l, tm=512, tn=512, name="ffn_down")
        y = _rmsnorm(x, norm_final, f32).reshape(nb, L, d)
        st[2] = _unpack_ssm_state(st[2])
        return (y,) + tuple(st)

    out_p = run_group(x_prompt, None, 0, 1)
    out_s = run_group(x_sample, (state_gdn, state_gdn_conv, ssm_in, state_ssm_conv, state_ret, state_ffn_conv),
                      PAST_LEN, SAMPLE_SEQS_PER_STEP)
    return (out_p[0], out_s[0]) + out_p[1:] + out_s[1:]
```

```python
import functools

import numpy as np
import jax
import jax.numpy as jnp
from jax import lax
from jax.experimental import pallas as pl
from jax.experimental.pallas import tpu as pltpu

f32 = jnp.float32
bf16 = jnp.bfloat16

D_MODEL = 4096
A_HEADS, A_DK, A_DV = 16, 128, 128
A_QKV = A_HEADS * (2 * A_DK + A_DV)
B_HEADS, B_HEADDIM, B_GROUPS, B_DSTATE = 32, 64, 4, 128
B_DINNER = B_HEADS * B_HEADDIM
B_REP = B_HEADS // B_GROUPS
B_PAIRS = B_HEADS // 2
B_CONV_DIM = B_DINNER + 2 * B_GROUPS * B_DSTATE
C_HEADS, C_DK, C_DV = 8, 256, 256
SHORT_CONV = 4
FFN_CONV = 3
CHUNK = 64
D_FF = 11008
ROPE_BASE = 10000.0
EPS = 1e-6
PAST_LEN = 16384
MIX_W = 2048

OFF_QKV = 0
OFF_AZ = 6144
OFF_BZ = 8192
OFF_BX = 10240
OFF_CQ = 12288
OFF_CK = 14336
OFF_CV = 16384
OFF_CG = 18432
OFF_GATES = 20480
OFF_BB = 32768
OFF_BC = 33280
OFF_SMALL = 33792
NP = 34304
LANE_AB, LANE_AA, LANE_BDT = 0, 16, 32

VMEM_LIMIT = 56 * 1024 * 1024
NEG_BIG = -1e30
SAMPLE_SEQS_PER_STEP = 2


def _cparams(*sem):
    return pltpu.CompilerParams(dimension_semantics=sem, vmem_limit_bytes=VMEM_LIMIT)


def _tile(n, pref, mult):
    t = min(pref, n)
    t -= t % mult
    while t >= mult:
        if n % t == 0:
            return t
        t -= mult
    return n


def _sigmoid(x):
    return 1.0 / (1.0 + jnp.exp(-x))


def _silu(x):
    return x * _sigmoid(x)


def _softplus(x):
    return jnp.maximum(x, 0.0) + jnp.log(1.0 + jnp.exp(-jnp.abs(x)))


def _dot(a, b):
    return jnp.dot(a.astype(bf16), b.astype(bf16), preferred_element_type=f32)


def _dot_nt(a, b):
    return lax.dot_general(a.astype(bf16), b.astype(bf16), (((1,), (1,)), ((), ())), preferred_element_type=f32)


def _dot_tn(a, b):
    return lax.dot_general(a.astype(bf16), b.astype(bf16), (((0,), (0,)), ((), ())), preferred_element_type=f32)


def _split(a):
    hi = lax.bitcast_convert_type(lax.bitcast_convert_type(a, jnp.int32) & jnp.int32(-65536), f32)
    return hi, a - hi


def _prep_l(sa):
    hi, lo = sa
    if hi.shape[1] == 64:
        return (jnp.concatenate([hi, lo, hi], axis=1).astype(bf16),)
    return hi.astype(bf16), lo.astype(bf16)


def _prep_r(sb):
    hi, lo = sb
    if hi.shape[0] == 64:
        return (jnp.concatenate([hi, hi, lo], axis=0).astype(bf16),)
    return hi.astype(bf16), lo.astype(bf16)


def _mm3(l, r):
    d = functools.partial(jnp.dot, preferred_element_type=f32)
    if len(l) == 1:
        return d(l[0], r[0])
    return d(l[0], r[0]) + (d(l[1], r[0]) + d(l[0], r[1]))


def _cumsum_rows(x, n):
    row = lax.broadcasted_iota(jnp.int32, x.shape, 0)
    s = 1
    while s < n:
        x = x + jnp.where(row >= s, pltpu.roll(x, s, axis=0), 0.0)
        s *= 2
    return x


def _col_to_row(col, eye):
    return jnp.sum(jnp.where(eye, col, 0.0), axis=0, keepdims=True)


def _inv_unit_lower(lmats, ii, jj, n):
    eye = (ii == jj).astype(f32)
    blk = (ii >> 3) == (jj >> 3)
    ps = [jnp.where(blk, -l, 0.0) for l in lmats]
    ts = [eye + p for p in ps]
    sp = [_split(p) for p in ps]
    p2 = [_mm3(_prep_l(s), _prep_r(s)) for s in sp]
    sp2 = [_split(q) for q in p2]
    p2r = [_prep_r(s) for s in sp2]
    ts = [t + _mm3(_prep_l(_split(t)), r) for t, r in zip(ts, p2r)]
    p4 = [_mm3(_prep_l(s), r) for s, r in zip(sp2, p2r)]
    ts = [t + _mm3(_prep_l(_split(t)), _prep_r(_split(q))) for t, q in zip(ts, p4)]
    s = 8
    while s < n:
        sh = s.bit_length() - 1
        mask = ((ii >> (sh + 1)) == (jj >> (sh + 1))) & ((ii >> sh) != (jj >> sh))
        st = [_split(t) for t in ts]
        mids = [_mm3(_prep_l(_split(jnp.where(mask, l, 0.0))), _prep_r(s_)) for l, s_ in zip(lmats, st)]
        ts = [t - _mm3(_prep_l(s_), _prep_r(_split(m))) for t, s_, m in zip(ts, st, mids)]
        s *= 2
    return ts


def _rmsnorm_kernel(x_ref, g_ref, o_ref):
    x = x_ref[...]
    y = x * lax.rsqrt(jnp.mean(x * x, axis=-1, keepdims=True) + EPS) * g_ref[...]
    o_ref[...] = y.astype(o_ref.dtype)


def _rmsnorm(x, gain, out_dtype):
    t, d = x.shape
    tm = _tile(t, 256, 16)
    return pl.pallas_call(
        _rmsnorm_kernel,
        grid=(t // tm,),
        in_specs=[pl.BlockSpec((tm, d), lambda i: (i, 0)), pl.BlockSpec((1, d), lambda i: (0, 0))],
        out_specs=pl.BlockSpec((tm, d), lambda i: (i, 0)),
        out_shape=jax.ShapeDtypeStruct((t, d), out_dtype),
        compiler_params=_cparams("parallel"),
        name="rmsnorm",
    )(x, gain.reshape(1, d))


def _mm_kernel(a_ref, b_ref, r_ref, o_ref):
    o_ref[...] = r_ref[...] + jnp.dot(a_ref[...], b_ref[...].astype(bf16), preferred_element_type=f32)


def _matmul(a, b, res, *, layer, tm, tn, name):
    m, kdim = a.shape
    n = b.shape[-1]
    tm = _tile(m, tm, 16)
    tn = _tile(n, tn, 128)
    return pl.pallas_call(
        _mm_kernel,
        grid=(m // tm, n // tn),
        in_specs=[pl.BlockSpec((tm, kdim), lambda i, j: (i, 0)),
                  pl.BlockSpec((None, kdim, tn), lambda i, j: (layer, 0, j)),
                  pl.BlockSpec((tm, tn), lambda i, j: (i, j))],
        out_specs=pl.BlockSpec((tm, tn), lambda i, j: (i, j)),
        out_shape=jax.ShapeDtypeStruct((m, n), f32),
        compiler_params=_cparams("parallel", "parallel"),
        name=name,
    )(a, b, res)


IN_PROJ_TN = 512


def _in_proj_kernel(n_main, offs_ref, u_ref, wt_ref, ws_ref, o_ref):
    j = pl.program_id(1)

    @pl.when(j < n_main)
    def _():
        o_ref[...] = _dot_nt(u_ref[...], wt_ref[0])

    @pl.when(j == n_main)
    def _():
        o_ref[...] = _dot_nt(u_ref[...], ws_ref[...])


def _in_proj(u, w_in_t, layer):
    o_ab = A_QKV + A_HEADS * A_DV
    o_bz = o_ab + 2 * A_HEADS
    o_bbc = o_bz + 2 * B_DINNER
    o_dt = o_bbc + 2 * B_GROUPS * B_DSTATE
    o_cq = o_dt + B_HEADS
    total = w_in_t.shape[1]
    tn = IN_PROJ_TN
    runs = [(OFF_QKV, 0, o_ab), (OFF_BZ, o_bz, o_bbc - o_bz), (OFF_CQ, o_cq, total - o_cq), (OFF_BB, o_bbc, o_dt - o_bbc)]
    offs = []
    for col0, src, width in runs:
        assert col0 == len(offs) * tn and width % tn == 0 and src % 32 == 0
        offs += [src + t * tn for t in range(width // tn)]
    assert len(offs) * tn == OFF_SMALL and NP == OFF_SMALL + tn
    n_main = len(offs)
    offs.append(0)
    m, kdim = u.shape
    tm = _tile(m, 1024, 16)
    small = jnp.concatenate([w_in_t[layer, o_ab:o_bz, :], w_in_t[layer, o_dt:o_cq, :],
                             jnp.zeros((tn - (o_bz - o_ab) - (o_cq - o_dt), kdim), w_in_t.dtype)], axis=0)
    grid_spec = pltpu.PrefetchScalarGridSpec(
        num_scalar_prefetch=1, grid=(m // tm, n_main + 1),
        in_specs=[pl.BlockSpec((pl.Element(tm), pl.Element(kdim)), lambda i, j, o: (i * tm, 0)),
                  pl.BlockSpec((pl.Element(1), pl.Element(tn), pl.Element(kdim)),
                               lambda i, j, o: (layer, pl.multiple_of(o[j], 32), 0)),
                  pl.BlockSpec((pl.Element(tn), pl.Element(kdim)), lambda i, j, o: (0, 0))],
        out_specs=pl.BlockSpec((tm, tn), lambda i, j, o: (i, j)))
    return pl.pallas_call(
        functools.partial(_in_proj_kernel, n_main), grid_spec=grid_spec,
        out_shape=jax.ShapeDtypeStruct((m, NP), f32),
        compiler_params=_cparams("parallel", "arbitrary"),
        name="in_proj",
    )(jnp.asarray(offs, jnp.int32), u, w_in_t, small)


def _merge_kernel(ya_ref, yb_ref, yc_ref, wa_ref, wb_ref, wc_ref, ga_ref, gb_ref, gc_ref, o_ref):
    d = functools.partial(jnp.dot, preferred_element_type=f32)
    h = _sigmoid(ga_ref[...]) * d(ya_ref[...], wa_ref[...])
    h = h + _sigmoid(gb_ref[...]) * d(yb_ref[...], wb_ref[...])
    h = h + _sigmoid(gc_ref[...]) * d(yc_ref[...], wc_ref[...])
    o_ref[...] = h.astype(o_ref.dtype)


def _merge(ya, yb, yc, wa, wb, wc, proj, layer):
    t, kd = ya.shape
    n = wa.shape[-1]
    tm = _tile(t, 1024, 16)
    tn = 512
    g0 = OFF_GATES // tn
    gstep = n // tn
    yspec = pl.BlockSpec((tm, kd), lambda i, j: (i, 0))
    wspec = pl.BlockSpec((None, kd, tn), lambda i, j: (layer, 0, j))
    gspecs = [pl.BlockSpec((tm, tn), functools.partial(lambda i, j, o: (i, o + j), o=g0 + r * gstep)) for r in range(3)]
    return pl.pallas_call(
        _merge_kernel,
        grid=(t // tm, n // tn),
        in_specs=[yspec, yspec, yspec, wspec, wspec, wspec] + gspecs,
        out_specs=pl.BlockSpec((tm, tn), lambda i, j: (i, j)),
        out_shape=jax.ShapeDtypeStruct((t, n), bf16),
        compiler_params=_cparams("parallel", "parallel"),
        name="merge",
    )(ya, yb, yc, wa, wb, wc, proj, proj, proj)


def _mixer_call(kernel_fn, name, proj_cols, consts, states_in, state_shapes, scratch, prev_states, *,
                proj, layer, depth, nb, L, C, NS):
    nch = L // C
    assert NS == 1 or nch == 1

    def rowmap(col):
        return lambda i, c: (i * nch + c, col)

    def stspec(nlayers, shp, l0):
        return pl.BlockSpec((nlayers, NS) + shp, functools.partial(lambda i, c, nd: (l0, i) + (0,) * nd, nd=len(shp)))

    rows = NS * C
    in_specs = [pl.BlockSpec((rows, w), rowmap(off // w)) for w, off in proj_cols]
    args = [proj] * len(proj_cols)
    for arr, shp in consts:
        in_specs.append(pl.BlockSpec(shp, functools.partial(lambda i, c, nd: (0,) * nd, nd=len(shp))))
        args.append(arr)
    for arr, shp in zip(states_in, state_shapes):
        in_specs.append(stspec(1, shp, layer))
        args.append(arr)
    aliases = {}
    if layer:
        for k, arr in enumerate(prev_states):
            aliases[len(args)] = 1 + k
            in_specs.append(pl.BlockSpec(memory_space=pl.ANY))
            args.append(arr)
        st_specs = [stspec(1, shp, layer) for shp in state_shapes]
    else:
        st_specs = [stspec(depth, shp, 0) for shp in state_shapes]
    return pl.pallas_call(
        functools.partial(kernel_fn, C, NS, bool(states_in), len(aliases)),
        grid=(nb // NS, nch),
        in_specs=in_specs,
        out_specs=[pl.BlockSpec((rows, MIX_W), rowmap(0))] + st_specs,
        out_shape=[jax.ShapeDtypeStruct((nb * L, MIX_W), bf16)]
        + [jax.ShapeDtypeStruct((depth, nb) + shp, f32) for shp in state_shapes],
        scratch_shapes=scratch + [pltpu.VMEM((rows, MIX_W), f32)],
        input_output_aliases=aliases,
        compiler_params=_cparams("parallel", "arbitrary"),
        name=name + ("_state" if states_in else ""),
    )(*args)


def _fill_layer_slots(c, *state_refs):
    @pl.when(c == pl.num_programs(1) - 1)
    def _():
        for ref in state_refs:
            for l in range(1, ref.shape[0]):
                ref[l] = ref[0]


def _gdn_kernel(C, NS, has_state, n_alias, *refs):
    qkv_ref, z_ref, sm_ref, cw_ref, par_ref, gn_ref = refs[:6]
    n_in = 6
    if has_state:
        s0_ref, cs_ref = refs[6:8]
        n_in = 8
    y_ref, so_ref, cso_ref, ext_ref, yacc_ref = refs[n_in + n_alias:]
    c = pl.program_id(1)

    @pl.when(c == 0)
    def _():
        for s in range(NS):
            ext_ref[s, 0:8, :] = jnp.zeros((8, A_QKV), f32)
            if has_state:
                ext_ref[s, 5:8, :] = cs_ref[0, s]
        so_ref[0] = s0_ref[0] if has_state else jnp.zeros(so_ref.shape[1:], f32)

    for s in range(NS):
        ext_ref[s, 8:8 + C, :] = qkv_ref[s * C:(s + 1) * C, :]

    def conv_silu(s, lo):
        acc = cw_ref[0:1, lo:lo + 128] * ext_ref[s, 5:5 + C, lo:lo + 128]
        for j in range(1, SHORT_CONV):
            acc = acc + cw_ref[j:j + 1, lo:lo + 128] * ext_ref[s, 5 + j:5 + j + C, lo:lo + 128]
        return _silu(acc)

    gates = []
    for s in range(NS):
        sm = sm_ref[s * C:(s + 1) * C, :]
        beta_all = _sigmoid(sm)
        g_all = -jnp.exp(par_ref[0:1, :]) * _softplus(sm + par_ref[1:2, :])
        gc_all = _cumsum_rows(g_all, C)
        gl_all = gc_all[C - 1:C, :]
        gates.append((beta_all, gc_all, jnp.exp(gc_all), jnp.exp(gl_all), jnp.exp(gl_all - gc_all)))

    ii = lax.broadcasted_iota(jnp.int32, (C, C), 0)
    jj = lax.broadcasted_iota(jnp.int32, (C, C), 1)
    eye = ii == jj
    tri = ii >= jj
    strict = ii > jj
    gain = gn_ref[...]

    def col(s, which, lane):
        return gates[s][which][:, lane:lane + 1]

    insts = [(s, h) for h in range(A_HEADS) for s in range(NS)]
    qs = [conv_silu(s, h * A_DK) for s, h in insts]
    qs = [q * lax.rsqrt(jnp.sum(q * q, axis=-1, keepdims=True) + EPS) * (A_DK ** -0.5) for q in qs]
    ks = [conv_silu(s, A_HEADS * A_DK + h * A_DK) for s, h in insts]
    ks = [k * lax.rsqrt(jnp.sum(k * k, axis=-1, keepdims=True) + EPS) for k in ks]
    vs = [conv_silu(s, 2 * A_HEADS * A_DK + h * A_DV) for s, h in insts]
    betas = [col(s, 0, LANE_AB + h) for s, h in insts]
    gcols = [col(s, 1, LANE_AA + h) for s, h in insts]
    egcs = [col(s, 2, LANE_AA + h) for s, h in insts]
    decays = [jnp.exp(jnp.where(tri, g - _col_to_row(g, eye), NEG_BIG)) for g in gcols]
    kbs = [k * b for k, b in zip(ks, betas)]
    lmats = [jnp.where(strict, _dot_nt(kb, k) * d, 0.0) for kb, k, d in zip(kbs, ks, decays)]
    rhss = [jnp.concatenate([v * b, kb * e], axis=-1) for v, b, kb, e in zip(vs, betas, kbs, egcs)]
    tinv = _inv_unit_lower(lmats, ii, jj, C)
    sols = [_mm3(_prep_l(_split(t)), _prep_r(_split(r))) for t, r in zip(tinv, rhss)]
    sts = [so_ref[0, s, h] for s, h in insts]
    vnews = [sol[:, :A_DV] - _dot(sol[:, A_DV:], st) for sol, st in zip(sols, sts)]
    qks = [jnp.where(tri, _dot_nt(q, k) * d, 0.0) for q, k, d in zip(qs, ks, decays)]
    outs = [_dot(q * e, st) + _dot(qk, vn) for q, e, st, qk, vn in zip(qs, egcs, sts, qks, vnews)]
    for (s, h), k, st, vn in zip(insts, ks, sts, vnews):
        so_ref[0, s, h] = st * col(s, 3, LANE_AA + h) + _dot_tn(k * col(s, 4, LANE_AA + h), vn)
    for (s, h), o in zip(insts, outs):
        o = o * lax.rsqrt(jnp.mean(o * o, axis=-1, keepdims=True) + EPS) * gain
        yacc_ref[s * C:(s + 1) * C, h * A_DV:(h + 1) * A_DV] = o * _silu(
            z_ref[s * C:(s + 1) * C, h * A_DV:(h + 1) * A_DV])
    y_ref[...] = yacc_ref[...].astype(y_ref.dtype)

    for s in range(NS):
        cso_ref[0, s] = ext_ref[s, C + 5:C + 8, :]
        ext_ref[s, 0:8, :] = ext_ref[s, C:C + 8, :]
    _fill_layer_slots(c, so_ref, cso_ref)


def _gdn(proj, cw, par, gn, states_in, prev_states, **kw):
    return _mixer_call(
        _gdn_kernel, "gdn",
        [(A_QKV, OFF_QKV), (MIX_W, OFF_AZ), (128, OFF_SMALL)],
        [(cw, (SHORT_CONV, A_QKV)), (par, (2, 128)), (gn, (1, A_DV))],
        states_in, [(A_HEADS, A_DK, A_DV), (SHORT_CONV - 1, A_QKV)],
        [pltpu.VMEM((kw["NS"], kw["C"] + 8, A_QKV), f32)], prev_states, proj=proj, **kw)


def _ssd_kernel(C, NS, has_state, n_alias, *refs):
    z_ref, x_ref, bb_ref, bc_ref, sm_ref, cw_ref, cb_ref, par_ref, gn_ref = refs[:9]
    n_in = 9
    if has_state:
        s0_ref, cs_ref = refs[9:11]
        n_in = 11
    y_ref, so_ref, cso_ref, ext_ref, yacc_ref = refs[n_in + n_alias:]
    c = pl.program_id(1)
    ng = B_GROUPS * B_DSTATE

    @pl.when(c == 0)
    def _():
        for s in range(NS):
            ext_ref[s, 0:8, :] = jnp.zeros((8, B_CONV_DIM), f32)
            if has_state:
                ext_ref[s, 5:8, :] = cs_ref[0, s]
        so_ref[0] = s0_ref[0] if has_state else jnp.zeros(so_ref.shape[1:], f32)

    ii = lax.broadcasted_iota(jnp.int32, (C, C), 0)
    jj = lax.broadcasted_iota(jnp.int32, (C, C), 1)
    eye = ii == jj
    tri = ii >= jj
    first = lax.broadcasted_iota(jnp.int32, (C, 128), 1) < B_HEADDIM
    first_lane = lax.broadcasted_iota(jnp.int32, (1, 128), 1) < B_HEADDIM
    first_row = lax.broadcasted_iota(jnp.int32, (2 * B_HEADDIM, 1), 0) < B_HEADDIM
    pairs_per_group = B_REP // 2

    for s in range(NS):
        r0 = s * C
        ext_ref[s, 8:8 + C, 0:B_DINNER] = x_ref[r0:r0 + C, :]
        ext_ref[s, 8:8 + C, B_DINNER:B_DINNER + ng] = bb_ref[r0:r0 + C, :]
        ext_ref[s, 8:8 + C, B_DINNER + ng:B_CONV_DIM] = bc_ref[r0:r0 + C, :]

    def conv_silu(s, lo):
        acc = cw_ref[0:1, lo:lo + 128] * ext_ref[s, 5:5 + C, lo:lo + 128]
        for j in range(1, SHORT_CONV):
            acc = acc + cw_ref[j:j + 1, lo:lo + 128] * ext_ref[s, 5 + j:5 + j + C, lo:lo + 128]
        return _silu(acc + cb_ref[0:1, lo:lo + 128])

    def pick(arr, l0, mask):
        return jnp.where(mask, arr[:, l0:l0 + 1], arr[:, l0 + 1:l0 + 2])

    gates = []
    for s in range(NS):
        sm = sm_ref[s * C:(s + 1) * C, :]
        dt_all = _softplus(sm + par_ref[1:2, :])
        gc_all = _cumsum_rows(dt_all * (-jnp.exp(par_ref[0:1, :])), C)
        gl_all = gc_all[C - 1:C, :]
        gates.append((dt_all, gc_all, jnp.exp(gc_all), jnp.exp(gl_all), jnp.exp(gl_all - gc_all)))
    d_all = par_ref[2:3, :]

    sg = [(s, g) for s in range(NS) for g in range(B_GROUPS)]
    bgs = {k: conv_silu(k[0], B_DINNER + k[1] * B_DSTATE) for k in sg}
    cgs = {k: conv_silu(k[0], B_DINNER + ng + k[1] * B_DSTATE) for k in sg}
    scores = {k: _dot_nt(cgs[k], bgs[k]) for k in sg}
    insts = [(s, p) for s in range(NS) for p in range(B_PAIRS)]
    lanes = [LANE_BDT + 2 * p for _, p in insts]
    grp = [(s, p // pairs_per_group) for s, p in insts]
    xps = [conv_silu(s, p * 128) for s, p in insts]
    vs = [xp * pick(gates[s][0], l0, first) for xp, (s, _), l0 in zip(xps, insts, lanes)]
    vbs = [v.astype(bf16) for v in vs]
    sds = []
    for (s, _), l0, k, vb in zip(insts, lanes, grp, vbs):
        pair = []
        for l in (l0, l0 + 1):
            gcol = gates[s][1][:, l:l + 1]
            decay = jnp.exp(jnp.where(tri, gcol - _col_to_row(gcol, eye), NEG_BIG))
            pair.append(jnp.dot((scores[k] * decay).astype(bf16), vb, preferred_element_type=f32))
        sds.append(pair)
    sts = [so_ref[0, s, p] for s, p in insts]
    outs = [jnp.where(first, sd[0], sd[1]) + _dot_nt(cgs[k], st) * pick(gates[s][2], l0, first)
            for sd, k, st, (s, _), l0 in zip(sds, grp, sts, insts, lanes)]
    for (s, p), l0, k, st, v in zip(insts, lanes, grp, sts, vs):
        so_ref[0, s, p] = st * pick(gates[s][3], l0, first_row) + _dot_tn(v * pick(gates[s][4], l0, first), bgs[k])
    yvs = [(o + pick(d_all, l0, first_lane) * xp) * _silu(z_ref[s * C:(s + 1) * C, p * 128:(p + 1) * 128])
           for o, l0, xp, (s, p) in zip(outs, lanes, xps, insts)]
    ssq = {k: jnp.zeros((C, 1), f32) for k in sg}
    for k, yv in zip(grp, yvs):
        ssq[k] = ssq[k] + jnp.sum(yv * yv, axis=-1, keepdims=True)
    scale = {k: lax.rsqrt(v * (1.0 / (B_DINNER // B_GROUPS)) + EPS) for k, v in ssq.items()}
    for (s, p), k, yv in zip(insts, grp, yvs):
        yacc_ref[s * C:(s + 1) * C, p * 128:(p + 1) * 128] = yv * scale[k] * gn_ref[0:1, p * 128:(p + 1) * 128]
    y_ref[...] = yacc_ref[...].astype(y_ref.dtype)

    for s in range(NS):
        cso_ref[0, s] = ext_ref[s, C + 5:C + 8, :]
        ext_ref[s, 0:8, :] = ext_ref[s, C:C + 8, :]
    _fill_layer_slots(c, so_ref, cso_ref)


def _ssd(proj, cw, cb, par, gn, states_in, prev_states, **kw):
    ng = B_GROUPS * B_DSTATE
    return _mixer_call(
        _ssd_kernel, "ssd",
        [(MIX_W, OFF_BZ), (MIX_W, OFF_BX), (ng, OFF_BB), (ng, OFF_BC), (128, OFF_SMALL)],
        [(cw, (SHORT_CONV, B_CONV_DIM)), (cb, (1, B_CONV_DIM)), (par, (3, 128)), (gn, (1, B_DINNER))],
        states_in, [(B_PAIRS, 2 * B_HEADDIM, B_DSTATE), (SHORT_CONV - 1, B_CONV_DIM)],
        [pltpu.VMEM((kw["NS"], kw["C"] + 8, B_CONV_DIM), f32)], prev_states, proj=proj, **kw)


def _pack_ssm_state(s):
    lead = s.shape[:-3]
    return jnp.swapaxes(s, -1, -2).reshape(lead + (B_PAIRS, 2 * B_HEADDIM, B_DSTATE))


def _unpack_ssm_state(s):
    lead = s.shape[:-3]
    return jnp.swapaxes(s.reshape(lead + (B_HEADS, B_HEADDIM, B_DSTATE)), -1, -2)


_LOG_GAMMA = [float(np.log(np.float32(1.0) - np.float32(2.0) ** np.float32(-5.0 - h))) for h in range(C_HEADS)]


def _ret_kernel(pos0, C, NS, has_state, n_alias, *refs):
    q_ref, k_ref, v_ref, g_ref, inv_ref = refs[:5]
    n_in = 5
    if has_state:
        s0_ref = refs[5]
        n_in = 6
    y_ref, so_ref, yacc_ref = refs[n_in + n_alias:]
    c = pl.program_id(1)

    @pl.when(c == 0)
    def _():
        so_ref[0] = s0_ref[0] if has_state else jnp.zeros(so_ref.shape[1:], f32)

    half = C_DK // 2
    pos = (pos0 + c * C + lax.broadcasted_iota(jnp.int32, (C, half), 0)).astype(f32)
    ang = pos * inv_ref[...]
    cos = jnp.cos(ang)
    sin = jnp.sin(ang)

    ii = lax.broadcasted_iota(jnp.int32, (C, C), 0)
    jj = lax.broadcasted_iota(jnp.int32, (C, C), 1)
    tri = ii >= jj
    dij = (ii - jj).astype(f32)
    ipos = (lax.broadcasted_iota(jnp.int32, (C, 1), 0) + 1).astype(f32)

    for s in range(NS):
        r0 = s * C

        def rot(ref, h, r0=r0):
            t1 = ref[r0:r0 + C, h * C_DK:h * C_DK + half]
            t2 = ref[r0:r0 + C, h * C_DK + half:(h + 1) * C_DK]
            return jnp.concatenate([t1 * cos - t2 * sin, t1 * sin + t2 * cos], axis=-1)

        for h in range(C_HEADS):
            lg = _LOG_GAMMA[h]
            q = rot(q_ref, h)
            k = rot(k_ref, h) * (C_DK ** -0.5)
            v = v_ref[r0:r0 + C, h * C_DV:(h + 1) * C_DV]
            decay = jnp.exp(jnp.where(tri, dij * lg, NEG_BIG))
            egc = jnp.exp(ipos * lg)
            edl = jnp.exp((float(C) - ipos) * lg)
            egl = float(np.exp(np.float32(C * lg)))
            st = so_ref[0, s, h]
            o = _dot(_dot_nt(q, k) * decay, v) + _dot(q, st) * egc
            so_ref[0, s, h] = st * egl + _dot_tn(k, v * edl)
            o = o * lax.rsqrt(jnp.mean(o * o, axis=-1, keepdims=True) + EPS)
            yacc_ref[r0:r0 + C, h * C_DV:(h + 1) * C_DV] = o * _silu(g_ref[r0:r0 + C, h * C_DV:(h + 1) * C_DV])
    y_ref[...] = yacc_ref[...].astype(y_ref.dtype)
    _fill_layer_slots(c, so_ref)


def _ret(proj, inv, states_in, prev_states, *, pos0, **kw):
    return _mixer_call(
        functools.partial(_ret_kernel, pos0), "ret",
        [(MIX_W, OFF_CQ), (MIX_W, OFF_CK), (MIX_W, OFF_CV), (MIX_W, OFF_CG)],
        [(inv, (1, C_DK // 2))],
        states_in, [(C_HEADS, C_DK, C_DV)], [], prev_states, proj=proj, **kw)


FFN_TN = 256


def _ffn_gu_prompt_kernel(tm, rchunk, tiles_per_seq, n_tiles, layer, *refs):
    u_ref, wg_ref, wu_ref, cw_ref, cb_ref = refs[:5]
    n_in = 5
    if layer:
        prev_cso_ref = refs[5]
        n_in = 6
    a_ref, cso_ref, wgb_ref, wub_ref, g0_ref, g1_ref, u0_ref, u1_ref = refs[n_in:]
    i = pl.program_id(1)

    @pl.when(i == 0)
    def _():
        wgb_ref[...] = wg_ref[...].astype(bf16)
        wub_ref[...] = wu_ref[...].astype(bf16)
        g1_ref[...] = jnp.zeros(g1_ref.shape, f32)
        u1_ref[...] = jnp.zeros(u1_ref.shape, f32)

    def finish(g_ref, up_ref, r0, rows):
        acc = cw_ref[0:1, :] * g_ref[6 + r0:6 + r0 + rows, :]
        for j in range(1, FFN_CONV):
            acc = acc + cw_ref[j:j + 1, :] * g_ref[6 + j + r0:6 + j + r0 + rows, :]
        acc = _silu(acc + cb_ref[...])
        a_ref[r0:r0 + rows, :] = (acc * up_ref[r0:r0 + rows, :]).astype(a_ref.dtype)

    def conv_state_out(g_ref):
        cso_ref[layer, 0] = g_ref[tm + 6:tm + 8, :]
        if layer:
            cso_ref[0:layer] = prev_cso_ref[...]

    pairs = ((g0_ref, u0_ref), (g1_ref, u1_ref))
    for slot in (0, 1):
        @pl.when((i < n_tiles) & (i % 2 == slot))
        def _(slot=slot):
            g_new, u_new = pairs[slot]
            g_old, u_old = pairs[1 - slot]
            conv_state_out(g_old)
            g_new[0:8, :] = jnp.where((i % tiles_per_seq) == 0, 0.0, g_old[tm:tm + 8, :])
            for r0 in range(0, tm, rchunk):
                u = u_ref[r0:r0 + rchunk, :]
                g_new[8 + r0:8 + r0 + rchunk, :] = jnp.dot(u, wgb_ref[...], preferred_element_type=f32)
                u_new[r0:r0 + rchunk, :] = jnp.dot(u, wub_ref[...], preferred_element_type=f32)
                finish(g_old, u_old, r0, rchunk)

    @pl.when(i == n_tiles)
    def _():
        g_last, u_last = pairs[(n_tiles - 1) % 2]
        conv_state_out(g_last)
        finish(g_last, u_last, 0, tm)


def _ffn_gu_sample_kernel(nb, L, layer, *refs):
    u_ref, wg_ref, wu_ref, cw_ref, cb_ref, cs_ref = refs[:6]
    n_in = 6
    if layer:
        prev_cso_ref = refs[6]
        n_in = 7
    a_ref, cso_ref, ext_ref = refs[n_in:]
    if layer:
        cso_ref[0:layer] = prev_cso_ref[...]
    u = u_ref[...]
    gate = jnp.dot(u, wg_ref[...].astype(bf16), preferred_element_type=f32)
    ext_ref[:, 0:8, :] = jnp.zeros((nb, 8, FFN_TN), f32)
    ext_ref[:, 6:8, :] = cs_ref[0]
    ext_ref[:, 8:8 + L, :] = gate.reshape(nb, L, FFN_TN)
    acc = cw_ref[0:1, :] * ext_ref[:, 6:6 + L, :]
    for j in range(1, FFN_CONV):
        acc = acc + cw_ref[j:j + 1, :] * ext_ref[:, 6 + j:6 + j + L, :]
    acc = _silu(acc + cb_ref[...])
    up = jnp.dot(u, wu_ref[...].astype(bf16), preferred_element_type=f32)
    a_ref[...] = (acc.reshape(nb * L, FFN_TN) * up).astype(a_ref.dtype)
    cso_ref[layer] = ext_ref[:, L + 6:L + 8, :]


def _ffn_gu(u2, wg, wu, cw, cb, cs, prev_state, *, layer, nb, L):
    d = u2.shape[1]
    nf = wg.shape[-1]
    tn = FFN_TN
    w_spec = pl.BlockSpec((None, d, tn), lambda j, i: (layer, 0, j))
    cw_spec = pl.BlockSpec((None, FFN_CONV, tn), lambda j, i: (layer, 0, j))
    cb_spec = pl.BlockSpec((None, 1, tn), lambda j, i: (layer, 0, j))
    args = [u2, wg, wu, cw, cb]
    if cs is None:
        tm = _tile(L, 1024, 16)
        tps = L // tm
        n_tiles = nb * tps
        grid = (nf // tn, n_tiles + 1)
        seq_of = lambda i: jnp.maximum(i - 1, 0) // tps
        in_specs = [pl.BlockSpec((tm, d), lambda j, i: (jnp.minimum(i, n_tiles - 1), 0)),
                    w_spec, w_spec, cw_spec, cb_spec]
        out_specs = [pl.BlockSpec((tm, tn), lambda j, i: (jnp.maximum(i - 1, 0), j)),
                     pl.BlockSpec((layer + 1, 1, FFN_CONV - 1, tn), lambda j, i: (0, seq_of(i), 0, j))]
        prev_spec = pl.BlockSpec((layer, 1, FFN_CONV - 1, tn), lambda j, i: (0, seq_of(i), 0, j))
        body = functools.partial(_ffn_gu_prompt_kernel, tm, _tile(tm, 256, 8), tps, n_tiles, layer)
        scratch = [pltpu.VMEM((d, tn), bf16), pltpu.VMEM((d, tn), bf16),
                   pltpu.VMEM((tm + 8, tn), f32), pltpu.VMEM((tm + 8, tn), f32),
                   pltpu.VMEM((tm, tn), f32), pltpu.VMEM((tm, tn), f32)]
        name = "ffn_gu"
    else:
        tm = nb * L
        assert L % 8 == 0
        grid = (nf // tn, 1)
        in_specs = [pl.BlockSpec((tm, d), lambda j, i: (0, 0)), w_spec, w_spec, cw_spec, cb_spec,
                    pl.BlockSpec((1, nb, FFN_CONV - 1, tn), lambda j, i: (layer, 0, 0, j))]
        out_specs = [pl.BlockSpec((tm, tn), lambda j, i: (0, j)),
                     pl.BlockSpec((layer + 1, nb, FFN_CONV - 1, tn), lambda j, i: (0, 0, 0, j))]
        prev_spec = pl.BlockSpec((layer, nb, FFN_CONV - 1, tn), lambda j, i: (0, 0, 0, j))
        args.append(cs)
        body = functools.partial(_ffn_gu_sample_kernel, nb, L, layer)
        scratch = [pltpu.VMEM((nb, L + 8, tn), f32)]
        name = "ffn_gu_state"
    if layer:
        in_specs.append(prev_spec)
        args.append(prev_state)
    return pl.pallas_call(
        body,
        grid=grid,
        in_specs=in_specs,
        out_specs=out_specs,
        out_shape=[jax.ShapeDtypeStruct((nb * L, nf), bf16),
                   jax.ShapeDtypeStruct((layer + 1, nb, FFN_CONV - 1, nf), f32)],
        scratch_shapes=scratch,
        compiler_params=_cparams("parallel", "arbitrary"),
        name=name,
    )(*args)


def _lane_row(vals, lane0):
    return jnp.zeros((128,), f32).at[lane0:lane0 + vals.shape[0]].set(vals.astype(f32))


def kernel(x_prompt, x_sample, state_gdn, state_gdn_conv, state_ssm, state_ssm_conv, state_ret, state_ffn_conv, norm_mix, w_in, gdn_conv_w, gdn_a_log, gdn_dt_bias, gdn_norm, ssm_conv_w, ssm_conv_b, ssm_a_log, ssm_dt_bias, ssm_d, ssm_norm, w_branch_a, w_branch_b, w_branch_c, w_out, norm_ffn, w_ffn_gate, w_ffn_up, ffn_conv_w, ffn_conv_b, w_ffn_down, norm_final):
    nbp, lp, d = x_prompt.shape
    nbs, ls, _ = x_sample.shape
    depth = w_in.shape[0]
    half = C_DK // 2
    inv = (ROPE_BASE ** (-jnp.arange(half, dtype=f32) / half)).reshape(1, half)
    ssm_in = _pack_ssm_state(state_ssm)
    w_in_t = jnp.swapaxes(w_in, 1, 2)
    wa_bf, wb_bf, wc_bf = w_branch_a.astype(bf16), w_branch_b.astype(bf16), w_branch_c.astype(bf16)
    wd_bf = w_ffn_down.astype(bf16)
    fcb = ffn_conv_b.reshape(depth, 1, D_FF)

    def run_group(x3, states, pos0, ns):
        nb, L, _ = x3.shape
        x = x3.reshape(nb * L, d)
        chunk = CHUNK if L % CHUNK == 0 else L
        st = [None] * 6

        def sin(*idx):
            return [] if states is None else [states[k] for k in idx]

        for l in range(depth):
            kw = dict(layer=l, depth=depth, nb=nb, L=L, C=chunk, NS=ns)
            u = _rmsnorm(x, norm_mix[l], bf16)
            proj = _in_proj(u, w_in_t, l)

            gdn_par = jnp.stack([_lane_row(gdn_a_log[l], LANE_AA), _lane_row(gdn_dt_bias[l], LANE_AA)])
            ya, st[0], st[1] = _gdn(proj, gdn_conv_w[l], gdn_par, gdn_norm[l].reshape(1, A_DV), sin(0, 1),
                                    [st[0], st[1]], **kw)
            ssd_par = jnp.stack([_lane_row(ssm_a_log[l], LANE_BDT), _lane_row(ssm_dt_bias[l], LANE_BDT),
                                 _lane_row(ssm_d[l], LANE_BDT)])
            yb, st[2], st[3] = _ssd(proj, ssm_conv_w[l], ssm_conv_b[l].reshape(1, B_CONV_DIM), ssd_par,
                                    ssm_norm[l].reshape(1, B_DINNER), sin(2, 3), [st[2], st[3]], **kw)
            yc, st[4] = _ret(proj, inv, sin(4), [st[4]], pos0=pos0, **kw)

            h = _merge(ya, yb, yc, wa_bf, wb_bf, wc_bf, proj, l)
            x = _matmul(h, w_out, x, layer=l, tm=1024, tn=512, name="out_proj")

            u2 = _rmsnorm(x, norm_ffn[l], bf16)
            act, st[5] = _ffn_gu(u2, w_ffn_gate, w_ffn_up, ffn_conv_w, fcb, None if states is None else states[5],
                                 st[5], layer=l, nb=nb, L=L)
            x = _matmul(act, wd_bf, x, layer=l, tm=512, tn=512, name="ffn_down")
        y = _rmsnorm(x, norm_final, f32).reshape(nb, L, d)
        st[2] = _unpack_ssm_state(st[2])
        return (y,) + tuple(st)

    out_p = run_group(x_prompt, None, 0, 1)
    out_s = run_group(x_sample, (state_gdn, state_gdn_conv, ssm_in, state_ssm_conv, state_ret, state_ffn_conv),
                      PAST_LEN, SAMPLE_SEQS_PER_STEP)
    return (out_p[0], out_s[0]) + out_p[1:] + out_s[1:]
```

```python
import functools

import numpy as np
import jax
import jax.numpy as jnp
from jax import lax
from jax.experimental import pallas as pl
from jax.experimental.pallas import tpu as pltpu

f32 = jnp.float32
bf16 = jnp.bfloat16

D_MODEL = 4096
A_HEADS, A_DK, A_DV = 16, 128, 128
A_QKV = A_HEADS * (2 * A_DK + A_DV)
B_HEADS, B_HEADDIM, B_GROUPS, B_DSTATE = 32, 64, 4, 128
B_DINNER = B_HEADS * B_HEADDIM
B_REP = B_HEADS // B_GROUPS
B_PAIRS = B_HEADS // 2
B_CONV_DIM = B_DINNER + 2 * B_GROUPS * B_DSTATE
C_HEADS, C_DK, C_DV = 8, 256, 256
SHORT_CONV = 4
FFN_CONV = 3
CHUNK = 64
D_FF = 11008
ROPE_BASE = 10000.0
EPS = 1e-6
PAST_LEN = 16384
MIX_W = 2048

OFF_QKV = 0
OFF_AZ = 6144
OFF_BZ = 8192
OFF_BX = 10240
OFF_CQ = 12288
OFF_CK = 14336
OFF_CV = 16384
OFF_CG = 18432
OFF_GATES = 20480
OFF_BB = 32768
OFF_BC = 33280
OFF_SMALL = 33792
NP = 34304
LANE_AB, LANE_AA, LANE_BDT = 0, 16, 32

VMEM_LIMIT = 56 * 1024 * 1024
NEG_BIG = -1e30
SAMPLE_SEQS_PER_STEP = 2


def _cparams(*sem):
    return pltpu.CompilerParams(dimension_semantics=sem, vmem_limit_bytes=VMEM_LIMIT)


def _tile(n, pref, mult):
    t = min(pref, n)
    t -= t % mult
    while t >= mult:
        if n % t == 0:
            return t
        t -= mult
    return n


def _sigmoid(x):
    return 1.0 / (1.0 + jnp.exp(-x))


def _silu(x):
    return x * _sigmoid(x)


def _softplus(x):
    return jnp.maximum(x, 0.0) + jnp.log(1.0 + jnp.exp(-jnp.abs(x)))


def _dot(a, b):
    return jnp.dot(a.astype(bf16), b.astype(bf16), preferred_element_type=f32)


def _dot_nt(a, b):
    return lax.dot_general(a.astype(bf16), b.astype(bf16), (((1,), (1,)), ((), ())), preferred_element_type=f32)


def _dot_tn(a, b):
    return lax.dot_general(a.astype(bf16), b.astype(bf16), (((0,), (0,)), ((), ())), preferred_element_type=f32)


def _split(a):
    hi = lax.bitcast_convert_type(lax.bitcast_convert_type(a, jnp.int32) & jnp.int32(-65536), f32)
    return hi, a - hi


def _prep_l(sa):
    hi, lo = sa
    if hi.shape[1] == 64:
        return (jnp.concatenate([hi, lo, hi], axis=1).astype(bf16),)
    return hi.astype(bf16), lo.astype(bf16)


def _prep_r(sb):
    hi, lo = sb
    if hi.shape[0] == 64:
        return (jnp.concatenate([hi, hi, lo], axis=0).astype(bf16),)
    return hi.astype(bf16), lo.astype(bf16)


def _mm3(l, r):
    d = functools.partial(jnp.dot, preferred_element_type=f32)
    if len(l) == 1:
        return d(l[0], r[0])
    return d(l[0], r[0]) + (d(l[1], r[0]) + d(l[0], r[1]))


def _cumsum_rows(x, n):
    row = lax.broadcasted_iota(jnp.int32, x.shape, 0)
    s = 1
    while s < n:
        x = x + jnp.where(row >= s, pltpu.roll(x, s, axis=0), 0.0)
        s *= 2
    return x


def _col_to_row(col, eye):
    return jnp.sum(jnp.where(eye, col, 0.0), axis=0, keepdims=True)


def _inv_unit_lower(lmats, ii, jj, n):
    eye = (ii == jj).astype(f32)
    blk = (ii >> 3) == (jj >> 3)
    ps = [jnp.where(blk, -l, 0.0) for l in lmats]
    ts = [eye + p for p in ps]
    sp = [_split(p) for p in ps]
    p2 = [_mm3(_prep_l(s), _prep_r(s)) for s in sp]
    sp2 = [_split(q) for q in p2]
    p2r = [_prep_r(s) for s in sp2]
    ts = [t + _mm3(_prep_l(_split(t)), r) for t, r in zip(ts, p2r)]
    p4 = [_mm3(_prep_l(s), r) for s, r in zip(sp2, p2r)]
    ts = [t + _mm3(_prep_l(_split(t)), _prep_r(_split(q))) for t, q in zip(ts, p4)]
    s = 8
    while s < n:
        sh = s.bit_length() - 1
        mask = ((ii >> (sh + 1)) == (jj >> (sh + 1))) & ((ii >> sh) != (jj >> sh))
        st = [_split(t) for t in ts]
        mids = [_mm3(_prep_l(_split(jnp.where(mask, l, 0.0))), _prep_r(s_)) for l, s_ in zip(lmats, st)]
        ts = [t - _mm3(_prep_l(s_), _prep_r(_split(m))) for t, s_, m in zip(ts, st, mids)]
        s *= 2
    return ts


def _rmsnorm_kernel(x_ref, g_ref, o_ref):
    x = x_ref[...]
    y = x * lax.rsqrt(jnp.mean(x * x, axis=-1, keepdims=True) + EPS) * g_ref[...]
    o_ref[...] = y.astype(o_ref.dtype)


def _rmsnorm(x, gain, out_dtype):
    t, d = x.shape
    tm = _tile(t, 256, 16)
    return pl.pallas_call(
        _rmsnorm_kernel,
        grid=(t // tm,),
        in_specs=[pl.BlockSpec((tm, d), lambda i: (i, 0)), pl.BlockSpec((1, d), lambda i: (0, 0))],
        out_specs=pl.BlockSpec((tm, d), lambda i: (i, 0)),
        out_shape=jax.ShapeDtypeStruct((t, d), out_dtype),
        compiler_params=_cparams("parallel"),
        name="rmsnorm",
    )(x, gain.reshape(1, d))


def _mm_kernel(a_ref, b_ref, r_ref, o_ref):
    o_ref[...] = r_ref[...] + jnp.dot(a_ref[...], b_ref[...].astype(bf16), preferred_element_type=f32)


def _matmul(a, b, res, *, layer, tm, tn, name):
    m, kdim = a.shape
    n = b.shape[-1]
    tm = _tile(m, tm, 16)
    tn = _tile(n, tn, 128)
    return pl.pallas_call(
        _mm_kernel,
        grid=(m // tm, n // tn),
        in_specs=[pl.BlockSpec((tm, kdim), lambda i, j: (i, 0)),
                  pl.BlockSpec((None, kdim, tn), lambda i, j: (layer, 0, j)),
                  pl.BlockSpec((tm, tn), lambda i, j: (i, j))],
        out_specs=pl.BlockSpec((tm, tn), lambda i, j: (i, j)),
        out_shape=jax.ShapeDtypeStruct((m, n), f32),
        compiler_params=_cparams("parallel", "parallel"),
        name=name,
    )(a, b, res)


IN_PROJ_TN = 512


def _in_proj_kernel(n_main, offs_ref, u_ref, wt_ref, ws_ref, o_ref):
    j = pl.program_id(1)

    @pl.when(j < n_main)
    def _():
        o_ref[...] = _dot_nt(u_ref[...], wt_ref[0])

    @pl.when(j == n_main)
    def _():
        o_ref[...] = _dot_nt(u_ref[...], ws_ref[...])


def _in_proj(u, w_in_t, layer):
    o_ab = A_QKV + A_HEADS * A_DV
    o_bz = o_ab + 2 * A_HEADS
    o_bbc = o_bz + 2 * B_DINNER
    o_dt = o_bbc + 2 * B_GROUPS * B_DSTATE
    o_cq = o_dt + B_HEADS
    total = w_in_t.shape[1]
    tn = IN_PROJ_TN
    runs = [(OFF_QKV, 0, o_ab), (OFF_BZ, o_bz, o_bbc - o_bz), (OFF_CQ, o_cq, total - o_cq), (OFF_BB, o_bbc, o_dt - o_bbc)]
    offs = []
    for col0, src, width in runs:
        assert col0 == len(offs) * tn and width % tn == 0 and src % 32 == 0
        offs += [src + t * tn for t in range(width // tn)]
    assert len(offs) * tn == OFF_SMALL and NP == OFF_SMALL + tn
    n_main = len(offs)
    offs.append(0)
    m, kdim = u.shape
    tm = _tile(m, 1024, 16)
    small = jnp.concatenate([w_in_t[layer, o_ab:o_bz, :], w_in_t[layer, o_dt:o_cq, :],
                             jnp.zeros((tn - (o_bz - o_ab) - (o_cq - o_dt), kdim), w_in_t.dtype)], axis=0)
    grid_spec = pltpu.PrefetchScalarGridSpec(
        num_scalar_prefetch=1, grid=(m // tm, n_main + 1),
        in_specs=[pl.BlockSpec((pl.Element(tm), pl.Element(kdim)), lambda i, j, o: (i * tm, 0)),
                  pl.BlockSpec((pl.Element(1), pl.Element(tn), pl.Element(kdim)),
                               lambda i, j, o: (layer, pl.multiple_of(o[j], 32), 0)),
                  pl.BlockSpec((pl.Element(tn), pl.Element(kdim)), lambda i, j, o: (0, 0))],
        out_specs=pl.BlockSpec((tm, tn), lambda i, j, o: (i, j)))
    return pl.pallas_call(
        functools.partial(_in_proj_kernel, n_main), grid_spec=grid_spec,
        out_shape=jax.ShapeDtypeStruct((m, NP), f32),
        compiler_params=_cparams("parallel", "arbitrary"),
        name="in_proj",
    )(jnp.asarray(offs, jnp.int32), u, w_in_t, small)


def _merge_kernel(ya_ref, yb_ref, yc_ref, wa_ref, wb_ref, wc_ref, ga_ref, gb_ref, gc_ref, o_ref):
    d = functools.partial(jnp.dot, preferred_element_type=f32)
    h = _sigmoid(ga_ref[...]) * d(ya_ref[...], wa_ref[...])
    h = h + _sigmoid(gb_ref[...]) * d(yb_ref[...], wb_ref[...])
    h = h + _sigmoid(gc_ref[...]) * d(yc_ref[...], wc_ref[...])
    o_ref[...] = h.astype(o_ref.dtype)


def _merge(ya, yb, yc, wa, wb, wc, proj, layer):
    t, kd = ya.shape
    n = wa.shape[-1]
    tm = _tile(t, 1024, 16)
    tn = 512
    g0 = OFF_GATES // tn
    gstep = n // tn
    yspec = pl.BlockSpec((tm, kd), lambda i, j: (i, 0))
    wspec = pl.BlockSpec((None, kd, tn), lambda i, j: (layer, 0, j))
    gspecs = [pl.BlockSpec((tm, tn), functools.partial(lambda i, j, o: (i, o + j), o=g0 + r * gstep)) for r in range(3)]
    return pl.pallas_call(
        _merge_kernel,
        grid=(t // tm, n // tn),
        in_specs=[yspec, yspec, yspec, wspec, wspec, wspec] + gspecs,
        out_specs=pl.BlockSpec((tm, tn), lambda i, j: (i, j)),
        out_shape=jax.ShapeDtypeStruct((t, n), bf16),
        compiler_params=_cparams("parallel", "parallel"),
        name="merge",
    )(ya, yb, yc, wa, wb, wc, proj, proj, proj)


def _mixer_call(kernel_fn, name, proj_cols, consts, states_in, state_shapes, scratch, prev_states, *,
                proj, layer, depth, nb, L, C, NS):
    nch = L // C
    assert NS == 1 or nch == 1

    def rowmap(col):
        return lambda i, c: (i * nch + c, col)

    def stspec(nlayers, shp, l0):
        return pl.BlockSpec((nlayers, NS) + shp, functools.partial(lambda i, c, nd: (l0, i) + (0,) * nd, nd=len(shp)))

    rows = NS * C
    in_specs = [pl.BlockSpec((rows, w), rowmap(off // w)) for w, off in proj_cols]
    args = [proj] * len(proj_cols)
    for arr, shp in consts:
        in_specs.append(pl.BlockSpec(shp, functools.partial(lambda i, c, nd: (0,) * nd, nd=len(shp))))
        args.append(arr)
    for arr, shp in zip(states_in, state_shapes):
        in_specs.append(stspec(1, shp, layer))
        args.append(arr)
    aliases = {}
    if layer:
        for k, arr in enumerate(prev_states):
            aliases[len(args)] = 1 + k
            in_specs.append(pl.BlockSpec(memory_space=pl.ANY))
            args.append(arr)
        st_specs = [stspec(1, shp, layer) for shp in state_shapes]
    else:
        st_specs = [stspec(depth, shp, 0) for shp in state_shapes]
    return pl.pallas_call(
        functools.partial(kernel_fn, C, NS, bool(states_in), len(aliases)),
        grid=(nb // NS, nch),
        in_specs=in_specs,
        out_specs=[pl.BlockSpec((rows, MIX_W), rowmap(0))] + st_specs,
        out_shape=[jax.ShapeDtypeStruct((nb * L, MIX_W), bf16)]
        + [jax.ShapeDtypeStruct((depth, nb) + shp, f32) for shp in state_shapes],
        scratch_shapes=scratch + [pltpu.VMEM((rows, MIX_W), f32)],
        input_output_aliases=aliases,
        compiler_params=_cparams("parallel", "arbitrary"),
        name=name + ("_state" if states_in else ""),
    )(*args)


def _fill_layer_slots(c, *state_refs):
    @pl.when(c == pl.num_programs(1) - 1)
    def _():
        for ref in state_refs:
            for l in range(1, ref.shape[0]):
                ref[l] = ref[0]


def _gdn_kernel(C, NS, has_state, n_alias, *refs):
    qkv_ref, z_ref, sm_ref, cw_ref, par_ref, gn_ref = refs[:6]
    n_in = 6
    if has_state:
        s0_ref, cs_ref = refs[6:8]
        n_in = 8
    y_ref, so_ref, cso_ref, ext_ref, yacc_ref = refs[n_in + n_alias:]
    c = pl.program_id(1)

    @pl.when(c == 0)
    def _():
        for s in range(NS):
            ext_ref[s, 0:8, :] = jnp.zeros((8, A_QKV), f32)
            if has_state:
                ext_ref[s, 5:8, :] = cs_ref[0, s]
        so_ref[0] = s0_ref[0] if has_state else jnp.zeros(so_ref.shape[1:], f32)

    for s in range(NS):
        ext_ref[s, 8:8 + C, :] = qkv_ref[s * C:(s + 1) * C, :]

    def conv_silu(s, lo):
        acc = cw_ref[0:1, lo:lo + 128] * ext_ref[s, 5:5 + C, lo:lo + 128]
        for j in range(1, SHORT_CONV):
            acc = acc + cw_ref[j:j + 1, lo:lo + 128] * ext_ref[s, 5 + j:5 + j + C, lo:lo + 128]
        return _silu(acc)

    gates = []
    for s in range(NS):
        sm = sm_ref[s * C:(s + 1) * C, :]
        beta_all = _sigmoid(sm)
        g_all = -jnp.exp(par_ref[0:1, :]) * _softplus(sm + par_ref[1:2, :])
        gc_all = _cumsum_rows(g_all, C)
        gl_all = gc_all[C - 1:C, :]
        gates.append((beta_all, gc_all, jnp.exp(gc_all), jnp.exp(gl_all), jnp.exp(gl_all - gc_all)))

    ii = lax.broadcasted_iota(jnp.int32, (C, C), 0)
    jj = lax.broadcasted_iota(jnp.int32, (C, C), 1)
    eye = ii == jj
    tri = ii >= jj
    strict = ii > jj
    gain = gn_ref[...]

    def col(s, which, lane):
        return gates[s][which][:, lane:lane + 1]

    insts = [(s, h) for h in range(A_HEADS) for s in range(NS)]
    qs = [conv_silu(s, h * A_DK) for s, h in insts]
    qs = [q * lax.rsqrt(jnp.sum(q * q, axis=-1, keepdims=True) + EPS) * (A_DK ** -0.5) for q in qs]
    ks = [conv_silu(s, A_HEADS * A_DK + h * A_DK) for s, h in insts]
    ks = [k * lax.rsqrt(jnp.sum(k * k, axis=-1, keepdims=True) + EPS) for k in ks]
    vs = [conv_silu(s, 2 * A_HEADS * A_DK + h * A_DV) for s, h in insts]
    betas = [col(s, 0, LANE_AB + h) for s, h in insts]
    gcols = [col(s, 1, LANE_AA + h) for s, h in insts]
    egcs = [col(s, 2, LANE_AA + h) for s, h in insts]
    decays = [jnp.exp(jnp.where(tri, g - _col_to_row(g, eye), NEG_BIG)) for g in gcols]
    kbs = [k * b for k, b in zip(ks, betas)]
    lmats = [jnp.where(strict, _dot_nt(kb, k) * d, 0.0) for kb, k, d in zip(kbs, ks, decays)]
    rhss = [jnp.concatenate([v * b, kb * e], axis=-1) for v, b, kb, e in zip(vs, betas, kbs, egcs)]
    tinv = _inv_unit_lower(lmats, ii, jj, C)
    sols = [_mm3(_prep_l(_split(t)), _prep_r(_split(r))) for t, r in zip(tinv, rhss)]
    sts = [so_ref[0, s, h] for s, h in insts]
    vnews = [sol[:, :A_DV] - _dot(sol[:, A_DV:], st) for sol, st in zip(sols, sts)]
    qks = [jnp.where(tri, _dot_nt(q, k) * d, 0.0) for q, k, d in zip(qs, ks, decays)]
    outs = [_dot(q * e, st) + _dot(qk, vn) for q, e, st, qk, vn in zip(qs, egcs, sts, qks, vnews)]
    for (s, h), k, st, vn in zip(insts, ks, sts, vnews):
        so_ref[0, s, h] = st * col(s, 3, LANE_AA + h) + _dot_tn(k * col(s, 4, LANE_AA + h), vn)
    for (s, h), o in zip(insts, outs):
        o = o * lax.rsqrt(jnp.mean(o * o, axis=-1, keepdims=True) + EPS) * gain
        yacc_ref[s * C:(s + 1) * C, h * A_DV:(h + 1) * A_DV] = o * _silu(
            z_ref[s * C:(s + 1) * C, h * A_DV:(h + 1) * A_DV])
    y_ref[...] = yacc_ref[...].astype(y_ref.dtype)

    for s in range(NS):
        cso_ref[0, s] = ext_ref[s, C + 5:C + 8, :]
        ext_ref[s, 0:8, :] = ext_ref[s, C:C + 8, :]
    _fill_layer_slots(c, so_ref, cso_ref)


def _gdn(proj, cw, par, gn, states_in, prev_states, **kw):
    return _mixer_call(
        _gdn_kernel, "gdn",
        [(A_QKV, OFF_QKV), (MIX_W, OFF_AZ), (128, OFF_SMALL)],
        [(cw, (SHORT_CONV, A_QKV)), (par, (2, 128)), (gn, (1, A_DV))],
        states_in, [(A_HEADS, A_DK, A_DV), (SHORT_CONV - 1, A_QKV)],
        [pltpu.VMEM((kw["NS"], kw["C"] + 8, A_QKV), f32)], prev_states, proj=proj, **kw)


def _ssd_kernel(C, NS, has_state, n_alias, *refs):
    z_ref, x_ref, bb_ref, bc_ref, sm_ref, cw_ref, cb_ref, par_ref, gn_ref = refs[:9]
    n_in = 9
    if has_state:
        s0_ref, cs_ref = refs[9:11]
        n_in = 11
    y_ref, so_ref, cso_ref, ext_ref, yacc_ref = refs[n_in + n_alias:]
    c = pl.program_id(1)
    ng = B_GROUPS * B_DSTATE

    @pl.when(c == 0)
    def _():
        for s in range(NS):
            ext_ref[s, 0:8, :] = jnp.zeros((8, B_CONV_DIM), f32)
            if has_state:
                ext_ref[s, 5:8, :] = cs_ref[0, s]
        so_ref[0] = s0_ref[0] if has_state else jnp.zeros(so_ref.shape[1:], f32)

    ii = lax.broadcasted_iota(jnp.int32, (C, C), 0)
    jj = lax.broadcasted_iota(jnp.int32, (C, C), 1)
    eye = ii == jj
    tri = ii >= jj
    first = lax.broadcasted_iota(jnp.int32, (C, 128), 1) < B_HEADDIM
    first_lane = lax.broadcasted_iota(jnp.int32, (1, 128), 1) < B_HEADDIM
    first_row = lax.broadcasted_iota(jnp.int32, (2 * B_HEADDIM, 1), 0) < B_HEADDIM
    pairs_per_group = B_REP // 2

    for s in range(NS):
        r0 = s * C
        ext_ref[s, 8:8 + C, 0:B_DINNER] = x_ref[r0:r0 + C, :]
        ext_ref[s, 8:8 + C, B_DINNER:B_DINNER + ng] = bb_ref[r0:r0 + C, :]
        ext_ref[s, 8:8 + C, B_DINNER + ng:B_CONV_DIM] = bc_ref[r0:r0 + C, :]

    def conv_silu(s, lo):
        acc = cw_ref[0:1, lo:lo + 128] * ext_ref[s, 5:5 + C, lo:lo + 128]
        for j in range(1, SHORT_CONV):
            acc = acc + cw_ref[j:j + 1, lo:lo + 128] * ext_ref[s, 5 + j:5 + j + C, lo:lo + 128]
        return _silu(acc + cb_ref[0:1, lo:lo + 128])

    def pick(arr, l0, mask):
        return jnp.where(mask, arr[:, l0:l0 + 1], arr[:, l0 + 1:l0 + 2])

    gates = []
    for s in range(NS):
        sm = sm_ref[s * C:(s + 1) * C, :]
        dt_all = _softplus(sm + par_ref[1:2, :])
        gc_all = _cumsum_rows(dt_all * (-jnp.exp(par_ref[0:1, :])), C)
        gl_all = gc_all[C - 1:C, :]
        gates.append((dt_all, gc_all, jnp.exp(gc_all), jnp.exp(gl_all), jnp.exp(gl_all - gc_all)))
    d_all = par_ref[2:3, :]

    sg = [(s, g) for s in range(NS) for g in range(B_GROUPS)]
    bgs = {k: conv_silu(k[0], B_DINNER + k[1] * B_DSTATE) for k in sg}
    cgs = {k: conv_silu(k[0], B_DINNER + ng + k[1] * B_DSTATE) for k in sg}
    scores = {k: _dot_nt(cgs[k], bgs[k]) for k in sg}
    insts = [(s, p) for s in range(NS) for p in range(B_PAIRS)]
    lanes = [LANE_BDT + 2 * p for _, p in insts]
    grp = [(s, p // pairs_per_group) for s, p in insts]
    xps = [conv_silu(s, p * 128) for s, p in insts]
    vs = [xp * pick(gates[s][0], l0, first) for xp, (s, _), l0 in zip(xps, insts, lanes)]
    vbs = [v.astype(bf16) for v in vs]
    sds = []
    for (s, _), l0, k, vb in zip(insts, lanes, grp, vbs):
        pair = []
        for l in (l0, l0 + 1):
            gcol = gates[s][1][:, l:l + 1]
            decay = jnp.exp(jnp.where(tri, gcol - _col_to_row(gcol, eye), NEG_BIG))
            pair.append(jnp.dot((scores[k] * decay).astype(bf16), vb, preferred_element_type=f32))
        sds.append(pair)
    sts = [so_ref[0, s, p] for s, p in insts]
    outs = [jnp.where(first, sd[0], sd[1]) + _dot_nt(cgs[k], st) * pick(gates[s][2], l0, first)
            for sd, k, st, (s, _), l0 in zip(sds, grp, sts, insts, lanes)]
    for (s, p), l0, k, st, v in zip(insts, lanes, grp, sts, vs):
        so_ref[0, s, p] = st * pick(gates[s][3], l0, first_row) + _dot_tn(v * pick(gates[s][4], l0, first), bgs[k])
    yvs = [(o + pick(d_all, l0, first_lane) * xp) * _silu(z_ref[s * C:(s + 1) * C, p * 128:(p + 1) * 128])
           for o, l0, xp, (s, p) in zip(outs, lanes, xps, insts)]
    ssq = {k: jnp.zeros((C, 1), f32) for k in sg}
    for k, yv in zip(grp, yvs):
        ssq[k] = ssq[k] + jnp.sum(yv * yv, axis=-1, keepdims=True)
    scale = {k: lax.rsqrt(v * (1.0 / (B_DINNER // B_GROUPS)) + EPS) for k, v in ssq.items()}
    for (s, p), k, yv in zip(insts, grp, yvs):
        yacc_ref[s * C:(s + 1) * C, p * 128:(p + 1) * 128] = yv * scale[k] * gn_ref[0:1, p * 128:(p + 1) * 128]
    y_ref[...] = yacc_ref[...].astype(y_ref.dtype)

    for s in range(NS):
        cso_ref[0, s] = ext_ref[s, C + 5:C + 8, :]
        ext_ref[s, 0:8, :] = ext_ref[s, C:C + 8, :]
    _fill_layer_slots(c, so_ref, cso_ref)


def _ssd(proj, cw, cb, par, gn, states_in, prev_states, **kw):
    ng = B_GROUPS * B_DSTATE
    return _mixer_call(
        _ssd_kernel, "ssd",
        [(MIX_W, OFF_BZ), (MIX_W, OFF_BX), (ng, OFF_BB), (ng, OFF_BC), (128, OFF_SMALL)],
        [(cw, (SHORT_CONV, B_CONV_DIM)), (cb, (1, B_CONV_DIM)), (par, (3, 128)), (gn, (1, B_DINNER))],
        states_in, [(B_PAIRS, 2 * B_HEADDIM, B_DSTATE), (SHORT_CONV - 1, B_CONV_DIM)],
        [pltpu.VMEM((kw["NS"], kw["C"] + 8, B_CONV_DIM), f32)], prev_states, proj=proj, **kw)


def _pack_ssm_state(s):
    lead = s.shape[:-3]
    return jnp.swapaxes(s, -1, -2).reshape(lead + (B_PAIRS, 2 * B_HEADDIM, B_DSTATE))


def _unpack_ssm_state(s):
    lead = s.shape[:-3]
    return jnp.swapaxes(s.reshape(lead + (B_HEADS, B_HEADDIM, B_DSTATE)), -1, -2)


_LOG_GAMMA = [float(np.log(np.float32(1.0) - np.float32(2.0) ** np.float32(-5.0 - h))) for h in range(C_HEADS)]


def _ret_kernel(pos0, C, NS, has_state, n_alias, *refs):
    q_ref, k_ref, v_ref, g_ref, inv_ref = refs[:5]
    n_in = 5
    if has_state:
        s0_ref = refs[5]
        n_in = 6
    y_ref, so_ref, yacc_ref = refs[n_in + n_alias:]
    c = pl.program_id(1)

    @pl.when(c == 0)
    def _():
        so_ref[0] = s0_ref[0] if has_state else jnp.zeros(so_ref.shape[1:], f32)

    half = C_DK // 2
    pos = (pos0 + c * C + lax.broadcasted_iota(jnp.int32, (C, half), 0)).astype(f32)
    ang = pos * inv_ref[...]
    cos = jnp.cos(ang)
    sin = jnp.sin(ang)

    ii = lax.broadcasted_iota(jnp.int32, (C, C), 0)
    jj = lax.broadcasted_iota(jnp.int32, (C, C), 1)
    tri = ii >= jj
    dij = (ii - jj).astype(f32)
    ipos = (lax.broadcasted_iota(jnp.int32, (C, 1), 0) + 1).astype(f32)

    for s in range(NS):
        r0 = s * C

        def rot(ref, h, r0=r0):
            t1 = ref[r0:r0 + C, h * C_DK:h * C_DK + half]
            t2 = ref[r0:r0 + C, h * C_DK + half:(h + 1) * C_DK]
            return jnp.concatenate([t1 * cos - t2 * sin, t1 * sin + t2 * cos], axis=-1)

        for h in range(C_HEADS):
            lg = _LOG_GAMMA[h]
            q = rot(q_ref, h)
            k = rot(k_ref, h) * (C_DK ** -0.5)
            v = v_ref[r0:r0 + C, h * C_DV:(h + 1) * C_DV]
            decay = jnp.exp(jnp.where(tri, dij * lg, NEG_BIG))
            egc = jnp.exp(ipos * lg)
            edl = jnp.exp((float(C) - ipos) * lg)
            egl = float(np.exp(np.float32(C * lg)))
            st = so_ref[0, s, h]
            o = _dot(_dot_nt(q, k) * decay, v) + _dot(q, st) * egc
            so_ref[0, s, h] = st * egl + _dot_tn(k, v * edl)
            o = o * lax.rsqrt(jnp.mean(o * o, axis=-1, keepdims=True) + EPS)
            yacc_ref[r0:r0 + C, h * C_DV:(h + 1) * C_DV] = o * _silu(g_ref[r0:r0 + C, h * C_DV:(h + 1) * C_DV])
    y_ref[...] = yacc_ref[...].astype(y_ref.dtype)
    _fill_layer_slots(c, so_ref)


def _ret(proj, inv, states_in, prev_states, *, pos0, **kw):
    return _mixer_call(
        functools.partial(_ret_kernel, pos0), "ret",
        [(MIX_W, OFF_CQ), (MIX_W, OFF_CK), (MIX_W, OFF_CV), (MIX_W, OFF_CG)],
        [(inv, (1, C_DK // 2))],
        states_in, [(C_HEADS, C_DK, C_DV)], [], prev_states, proj=proj, **kw)


FFN_TN = 256


def _ffn_gu_prompt_kernel(tm, tiles_per_seq, layer, *refs):
    u_ref, wg_ref, wu_ref, cw_ref, cb_ref = refs[:5]
    n_in = 5
    if layer:
        prev_cso_ref = refs[5]
        n_in = 6
    a_ref, cso_ref, wgb_ref, wub_ref, ext_ref = refs[n_in:]
    i = pl.program_id(1)

    @pl.when(i == 0)
    def _():
        wgb_ref[...] = wg_ref[...].astype(bf16)
        wub_ref[...] = wu_ref[...].astype(bf16)

    @pl.when(i % tiles_per_seq == 0)
    def _():
        ext_ref[0:8, :] = jnp.zeros((8, FFN_TN), f32)

    u = u_ref[...]
    ext_ref[8:8 + tm, :] = jnp.dot(u, wgb_ref[...], preferred_element_type=f32)
    acc = cw_ref[0:1, :] * ext_ref[6:6 + tm, :]
    for j in range(1, FFN_CONV):
        acc = acc + cw_ref[j:j + 1, :] * ext_ref[6 + j:6 + j + tm, :]
    acc = _silu(acc + cb_ref[...])
    a_ref[...] = (acc * jnp.dot(u, wub_ref[...], preferred_element_type=f32)).astype(a_ref.dtype)
    cso_ref[layer, 0] = ext_ref[tm + 6:tm + 8, :]
    if layer:
        cso_ref[0:layer] = prev_cso_ref[...]
    ext_ref[0:8, :] = ext_ref[tm:tm + 8, :]


def _ffn_gu_sample_kernel(nb, L, layer, cast_down, *refs):
    u_ref, wg_ref, wu_ref, cw_ref, cb_ref, cs_ref = refs[:6]
    n_in = 6
    if layer:
        prev_cso_ref = refs[6]
        n_in = 7
    if cast_down:
        wd_ref = refs[n_in]
        a_ref, cso_ref, wdb_ref, ext_ref = refs[n_in + 1:]
        wdb_ref[...] = wd_ref[...].astype(bf16)
    else:
        a_ref, cso_ref, ext_ref = refs[n_in:]
    if layer:
        cso_ref[0:layer] = prev_cso_ref[...]
    u = u_ref[...]
    gate = jnp.dot(u, wg_ref[...].astype(bf16), preferred_element_type=f32)
    ext_ref[:, 0:8, :] = jnp.zeros((nb, 8, FFN_TN), f32)
    ext_ref[:, 6:8, :] = cs_ref[0]
    ext_ref[:, 8:8 + L, :] = gate.reshape(nb, L, FFN_TN)
    acc = cw_ref[0:1, :] * ext_ref[:, 6:6 + L, :]
    for j in range(1, FFN_CONV):
        acc = acc + cw_ref[j:j + 1, :] * ext_ref[:, 6 + j:6 + j + L, :]
    acc = _silu(acc + cb_ref[...])
    up = jnp.dot(u, wu_ref[...].astype(bf16), preferred_element_type=f32)
    a_ref[...] = (acc.reshape(nb * L, FFN_TN) * up).astype(a_ref.dtype)
    cso_ref[layer] = ext_ref[:, L + 6:L + 8, :]


def _ffn_gu(u2, wg, wu, cw, cb, cs, prev_state, *, layer, nb, L, wd=None):
    d = u2.shape[1]
    nf = wg.shape[-1]
    tn = FFN_TN
    w_spec = pl.BlockSpec((None, d, tn), lambda j, i: (layer, 0, j))
    cw_spec = pl.BlockSpec((None, FFN_CONV, tn), lambda j, i: (layer, 0, j))
    cb_spec = pl.BlockSpec((None, 1, tn), lambda j, i: (layer, 0, j))
    args = [u2, wg, wu, cw, cb]
    if cs is None:
        tm = _tile(L, 1024, 16)
        tps = L // tm
        grid = (nf // tn, nb * tps)
        in_specs = [pl.BlockSpec((tm, d), lambda j, i: (i, 0)), w_spec, w_spec, cw_spec, cb_spec]
        out_specs = [pl.BlockSpec((tm, tn), lambda j, i: (i, j)),
                     pl.BlockSpec((layer + 1, 1, FFN_CONV - 1, tn), lambda j, i: (0, i // tps, 0, j))]
        prev_spec = pl.BlockSpec((layer, 1, FFN_CONV - 1, tn), lambda j, i: (0, i // tps, 0, j))
        body = functools.partial(_ffn_gu_prompt_kernel, tm, tps, layer)
        scratch = [pltpu.VMEM((d, tn), bf16), pltpu.VMEM((d, tn), bf16), pltpu.VMEM((tm + 8, tn), f32)]
        name = "ffn_gu"
    else:
        tm = nb * L
        assert L % 8 == 0
        grid = (nf // tn, 1)
        in_specs = [pl.BlockSpec((tm, d), lambda j, i: (0, 0)), w_spec, w_spec, cw_spec, cb_spec,
                    pl.BlockSpec((1, nb, FFN_CONV - 1, tn), lambda j, i: (layer, 0, 0, j))]
        out_specs = [pl.BlockSpec((tm, tn), lambda j, i: (0, j)),
                     pl.BlockSpec((layer + 1, nb, FFN_CONV - 1, tn), lambda j, i: (0, 0, 0, j))]
        prev_spec = pl.BlockSpec((layer, nb, FFN_CONV - 1, tn), lambda j, i: (0, 0, 0, j))
        args.append(cs)
        body = functools.partial(_ffn_gu_sample_kernel, nb, L, layer, wd is not None)
        scratch = [pltpu.VMEM((nb, L + 8, tn), f32)]
        name = "ffn_gu_state"
    if layer:
        in_specs.append(prev_spec)
        args.append(prev_state)
    out_shape = [jax.ShapeDtypeStruct((nb * L, nf), bf16),
                 jax.ShapeDtypeStruct((layer + 1, nb, FFN_CONV - 1, nf), f32)]
    if wd is not None:
        assert cs is not None and wd.shape[1] == nf
        in_specs.append(pl.BlockSpec((None, tn, wd.shape[2]), lambda j, i: (layer, j, 0)))
        args.append(wd)
        out_specs.append(pl.BlockSpec((tn, wd.shape[2]), lambda j, i: (j, 0)))
        out_shape.append(jax.ShapeDtypeStruct(wd.shape[1:], bf16))
    return pl.pallas_call(
        body,
        grid=grid,
        in_specs=in_specs,
        out_specs=out_specs,
        out_shape=out_shape,
        scratch_shapes=scratch,
        compiler_params=_cparams("parallel", "arbitrary"),
        name=name,
    )(*args)


def _lane_row(vals, lane0):
    return jnp.zeros((128,), f32).at[lane0:lane0 + vals.shape[0]].set(vals.astype(f32))


def kernel(x_prompt, x_sample, state_gdn, state_gdn_conv, state_ssm, state_ssm_conv, state_ret, state_ffn_conv, norm_mix, w_in, gdn_conv_w, gdn_a_log, gdn_dt_bias, gdn_norm, ssm_conv_w, ssm_conv_b, ssm_a_log, ssm_dt_bias, ssm_d, ssm_norm, w_branch_a, w_branch_b, w_branch_c, w_out, norm_ffn, w_ffn_gate, w_ffn_up, ffn_conv_w, ffn_conv_b, w_ffn_down, norm_final):
    nbp, lp, d = x_prompt.shape
    nbs, ls, _ = x_sample.shape
    depth = w_in.shape[0]
    half = C_DK // 2
    inv = (ROPE_BASE ** (-jnp.arange(half, dtype=f32) / half)).reshape(1, half)
    ssm_in = _pack_ssm_state(state_ssm)
    w_in_t = jnp.swapaxes(w_in, 1, 2)
    wa_bf, wb_bf, wc_bf = w_branch_a.astype(bf16), w_branch_b.astype(bf16), w_branch_c.astype(bf16)
    wd_bf = [None] * depth
    fcb = ffn_conv_b.reshape(depth, 1, D_FF)

    def run_group(x3, states, pos0, ns):
        nb, L, _ = x3.shape
        x = x3.reshape(nb * L, d)
        chunk = CHUNK if L % CHUNK == 0 else L
        st = [None] * 6

        def sin(*idx):
            return [] if states is None else [states[k] for k in idx]

        for l in range(depth):
            kw = dict(layer=l, depth=depth, nb=nb, L=L, C=chunk, NS=ns)
            u = _rmsnorm(x, norm_mix[l], bf16)
            proj = _in_proj(u, w_in_t, l)

            gdn_par = jnp.stack([_lane_row(gdn_a_log[l], LANE_AA), _lane_row(gdn_dt_bias[l], LANE_AA)])
            ya, st[0], st[1] = _gdn(proj, gdn_conv_w[l], gdn_par, gdn_norm[l].reshape(1, A_DV), sin(0, 1),
                                    [st[0], st[1]], **kw)
            ssd_par = jnp.stack([_lane_row(ssm_a_log[l], LANE_BDT), _lane_row(ssm_dt_bias[l], LANE_BDT),
                                 _lane_row(ssm_d[l], LANE_BDT)])
            yb, st[2], st[3] = _ssd(proj, ssm_conv_w[l], ssm_conv_b[l].reshape(1, B_CONV_DIM), ssd_par,
                                    ssm_norm[l].reshape(1, B_DINNER), sin(2, 3), [st[2], st[3]], **kw)
            yc, st[4] = _ret(proj, inv, sin(4), [st[4]], pos0=pos0, **kw)

            h = _merge(ya, yb, yc, wa_bf, wb_bf, wc_bf, proj, l)
            x = _matmul(h, w_out, x, layer=l, tm=1024, tn=512, name="out_proj")

            u2 = _rmsnorm(x, norm_ffn[l], bf16)
            if states is None:
                act, st[5] = _ffn_gu(u2, w_ffn_gate, w_ffn_up, ffn_conv_w, fcb, None, st[5], layer=l, nb=nb, L=L)
            else:
                act, st[5], wd_bf[l] = _ffn_gu(u2, w_ffn_gate, w_ffn_up, ffn_conv_w, fcb, states[5], st[5],
                                               layer=l, nb=nb, L=L, wd=w_ffn_down)
            x = _matmul(act, wd_bf[l][None], x, layer=0, tm=512, tn=512, name="ffn_down")
        y = _rmsnorm(x, norm_final, f32).reshape(nb, L, d)
        st[2] = _unpack_ssm_state(st[2])
        return (y,) + tuple(st)

    out_s = run_group(x_sample, (state_gdn, state_gdn_conv, ssm_in, state_ssm_conv, state_ret, state_ffn_conv),
                      PAST_LEN, SAMPLE_SEQS_PER_STEP)
    out_p = run_group(x_prompt, None, 0, 1)
    return (out_p[0], out_s[0]) + out_p[1:] + out_s[1:]
```
